```python
import math
import jax, jax.numpy as jnp
from jax import lax
import numpy as np


D_MODEL = 4096
BATCH = 4
SEQ = 2048
DEPTH = 2
DEC_BATCH = 8
DEC_SEQ = 8
PAST_LEN = 16384
PAGE_SIZE = 128

D_MIX = D_MODEL
D_RNN = D_MIX // 2
N_LRU_BLOCKS = 16
LRU_BLOCK = D_RNN // N_LRU_BLOCKS
CONV_W = 4
LRU_C = 8.0
D_ATT = D_MIX - D_RNN
HEAD_DIM = 128
N_HEADS = D_ATT // (2 * HEAD_DIM)
V_DIM = 2 * HEAD_DIM
D_QK = N_HEADS * 2 * HEAD_DIM
D_IN = 2 * D_RNN + 2 * D_QK + N_HEADS * V_DIM
ROPE_THETA = 10000.0
D_FF = (D_MODEL * 7) // 2
N_EXPERTS = 8
TOP_K = 2
D_FF_EXPERT = (D_MODEL * 7) // 2
N_DENSE = (DEPTH + 1) // 2
N_MOE = DEPTH // 2
EPS = 1e-6
Q_BLOCK = 128
NEG_INF = -1e30

kernel_name = 'hymba_rglru_diffattn_moe_step'


def _rmsnorm(x, g):
    xf = x.astype(jnp.float32)
    y = xf * lax.rsqrt(jnp.mean(xf * xf, axis=-1, keepdims=True) + EPS)
    return (y * g.astype(jnp.float32)).astype(x.dtype)


def _rope(x, pos):
    half = HEAD_DIM // 2
    inv_freq = ROPE_THETA ** (-jnp.arange(half, dtype=jnp.float32) / half)
    ang = pos.astype(jnp.float32)[:, None] * inv_freq[None, :]
    cos = jnp.cos(ang)[None, :, None, None, :]
    sin = jnp.sin(ang)[None, :, None, None, :]
    xf = x.astype(jnp.float32)
    x1, x2 = xf[..., :half], xf[..., half:]
    out = jnp.concatenate([x1 * cos - x2 * sin, x2 * cos + x1 * sin], axis=-1)
    return out.astype(x.dtype)


def _causal_conv(x, hist, w, b):
    T = x.shape[1]
    xp = jnp.concatenate([hist.astype(x.dtype), x], axis=1)
    y = b[None, None, :] + w[0][None, None, :] * xp[:, 0:T]
    for j in range(1, CONV_W):
        y = y + w[j][None, None, :] * xp[:, j:j + T]
    return y, xp[:, T:]


def _rg_lru(x, wa, ba, wx, bx, lam, h0):
    B, T, _ = x.shape
    xb = x.reshape(B, T, N_LRU_BLOCKS, LRU_BLOCK)
    r = jax.nn.sigmoid((jnp.einsum('btni,nij->btnj', xb, wa).reshape(B, T, D_RNN) + ba).astype(jnp.float32))
    gi = jax.nn.sigmoid((jnp.einsum('btni,nij->btnj', xb, wx).reshape(B, T, D_RNN) + bx).astype(jnp.float32))
    log_a = -LRU_C * r * jax.nn.softplus(-lam.astype(jnp.float32))
    a = jnp.exp(log_a)
    bterm = jnp.sqrt(-jnp.expm1(2.0 * log_a)) * (gi * x.astype(jnp.float32))
    bterm = bterm.at[:, 0].add(a[:, 0] * h0.astype(jnp.float32))

    def comb(left, right):
        a1, b1 = left
        a2, b2 = right
        return a1 * a2, a2 * b1 + b2

    _, h = lax.associative_scan(comb, (a, bterm), axis=1)
    return h.astype(x.dtype), h[:, -1].astype(x.dtype)


def _diff_attention(q, k, v, q_pos, k_pos, lam):
    B, Tq = q.shape[0], q.shape[1]
    blk = Q_BLOCK if Tq % Q_BLOCK == 0 else Tq
    nb = Tq // blk
    qb = jnp.moveaxis(q.reshape(B, nb, blk, N_HEADS, 2, HEAD_DIM), 1, 0)
    pb = q_pos.reshape(nb, blk)
    scale = HEAD_DIM ** -0.5

    def one_block(args):
        qi, pi = args
        s = jnp.einsum('bqhcd,bkhcd->bhcqk', qi, k).astype(jnp.float32) * scale
        mask = k_pos[None, :] <= pi[:, None]
        s = jnp.where(mask, s, NEG_INF)
        p = jax.nn.softmax(s, axis=-1)
        wgt = (p[:, :, 0] - lam * p[:, :, 1]).astype(v.dtype)
        return jnp.einsum('bhqk,bkhe->bqhe', wgt, v)

    o = lax.map(one_block, (qb, pb))
    return jnp.moveaxis(o, 0, 1).reshape(B, Tq, N_HEADS, V_DIM)


def _mixer(h, pos, conv_hist, h0, k_past, v_past, p, l):
    B, T, _ = h.shape
    proj = h @ p['w_in'][l]
    xr = proj[..., :D_RNN]
    gr = proj[..., D_RNN:2 * D_RNN]
    q = proj[..., 2 * D_RNN:2 * D_RNN + D_QK].reshape(B, T, N_HEADS, 2, HEAD_DIM)
    k = proj[..., 2 * D_RNN + D_QK:2 * D_RNN + 2 * D_QK].reshape(B, T, N_HEADS, 2, HEAD_DIM)
    v = proj[..., 2 * D_RNN + 2 * D_QK:].reshape(B, T, N_HEADS, V_DIM)
    xc, conv_new = _causal_conv(xr, conv_hist, p['conv_w'][l], p['conv_b'][l])
    hs, h_last = _rg_lru(xc, p['lru_wa'][l], p['lru_ba'][l], p['lru_wx'][l], p['lru_bx'][l], p['lru_lambda'][l], h0)
    y_rnn = hs * jax.nn.gelu(gr)
    q = _rope(q, pos)
    k = _rope(k, pos)
    lam_init = 0.8 - 0.6 * math.exp(-0.3 * l)
    f32 = jnp.float32
    lam = (jnp.exp(jnp.sum(p['lambda_q1'][l].astype(f32) * p['lambda_k1'][l].astype(f32)))
           - jnp.exp(jnp.sum(p['lambda_q2'][l].astype(f32) * p['lambda_k2'][l].astype(f32))) + lam_init)
    if k_past is None:
        k_all, v_all, k_pos = k, v, pos
    else:
        k_all = jnp.concatenate([k_past.astype(k.dtype), k], axis=1)
        v_all = jnp.concatenate([v_past.astype(v.dtype), v], axis=1)
        k_pos = jnp.concatenate([jnp.arange(k_past.shape[1], dtype=jnp.int32), pos])
    o = _diff_attention(q, k_all, v_all, pos, k_pos, lam)
    o = _rmsnorm(o, p['subln_g'][l]) * (1.0 - lam_init)
    mix = jnp.concatenate([y_rnn, o.reshape(B, T, D_ATT)], axis=-1) @ p['w_out'][l]
    return mix, (k, v, h_last, conv_new)


def _swiglu(h, wg, wu, wd):
    return (jax.nn.silu(h @ wg) * (h @ wu)) @ wd


def _moe(h, rw, rb, wg, wu, wd):
    logits = (h @ rw + rb).astype(jnp.float32)
    top_v, top_i = lax.top_k(logits, TOP_K)
    gates = jax.nn.softmax(top_v, axis=-1)
    combine = jnp.sum(jax.nn.one_hot(top_i, N_EXPERTS, dtype=jnp.float32) * gates[..., None], axis=-2)
    out = jnp.zeros(h.shape, jnp.float32)
    for e in range(N_EXPERTS):
        out = out + combine[..., e:e + 1] * _swiglu(h, wg[e], wu[e], wd[e]).astype(jnp.float32)
    return out.astype(h.dtype)


def _trunk(x, c, pos, conv_hist, h0, kv_past, p):
    new_rows = []
    for l in range(DEPTH):
        mod = (jax.nn.silu(c) @ p['w_mod'][l] + p['b_mod'][l])[:, None, :]
        sh1, sc1, g1, sh2, sc2, g2 = jnp.split(mod, 6, axis=-1)
        hm = _rmsnorm(x, p['g_pre_mix'][l]) * (1.0 + sc1) + sh1
        k_past, v_past = (None, None) if kv_past is None else kv_past(l)
        m, st = _mixer(hm, pos, conv_hist[l], h0[l], k_past, v_past, p, l)
        x = x + g1 * _rmsnorm(m, p['g_post_mix'][l])
        hf = _rmsnorm(x, p['g_pre_ffn'][l]) * (1.0 + sc2) + sh2
        if l % 2 == 0:
            i = l // 2
            f = _swiglu(hf, p['w_gate'][i], p['w_up'][i], p['w_down'][i])
        else:
            i = l // 2
            f = _moe(hf, p['router_w'][i], p['router_b'][i], p['we_gate'][i], p['we_up'][i], p['we_down'][i])
        x = x + g2 * _rmsnorm(f, p['g_post_ffn'][l])
        new_rows.append(st)
    ks = jnp.stack([s[0] for s in new_rows])
    vs = jnp.stack([s[1] for s in new_rows])
    hs = jnp.stack([s[2] for s in new_rows])
    cs = jnp.stack([s[3] for s in new_rows])
    return x, ks, vs, hs, cs


def setup_inputs(seed: int = 0) -> dict:
    key = jax.random.key(seed)
    ks = iter(jax.random.split(key, 48))
    f32 = jnp.float32

    def nrm(shape, scale):
        return jax.random.normal(next(ks), shape, f32) * scale

    n_pages = PAST_LEN // PAGE_SIZE
    n_used = DEC_BATCH * n_pages
    n_pool = n_used + max(1, n_used // 4)
    perm = jax.random.permutation(next(ks), n_pool).astype(jnp.int32)
    page_table = perm[:n_used].reshape(DEC_BATCH, n_pages)
    u = jax.random.uniform(next(ks), (DEPTH, D_RNN), f32, 0.9, 0.999)
    s = u ** (1.0 / LRU_C)
    lru_lambda = jnp.log(s) - jnp.log1p(-s)
    return {
        'x_prompt': nrm((BATCH, SEQ, D_MODEL), 1.0),
        'x_sample': nrm((DEC_BATCH, DEC_SEQ, D_MODEL), 1.0),
        'c_prompt': nrm((BATCH, D_MODEL), 1.0),
        'c_sample': nrm((DEC_BATCH, D_MODEL), 1.0),
        'cache_k': nrm((DEPTH, n_pool, PAGE_SIZE, N_HEADS, 2, HEAD_DIM), 1.0),
        'cache_v': nrm((DEPTH, n_pool, PAGE_SIZE, N_HEADS, V_DIM), 1.0),
        'page_table': page_table,
        'state_h': nrm((DEPTH, DEC_BATCH, D_RNN), 0.5),
        'state_conv': nrm((DEPTH, DEC_BATCH, CONV_W - 1, D_RNN), 1.0),
        'w_in': nrm((DEPTH, D_MODEL, D_IN), D_MODEL ** -0.5),
        'conv_w': nrm((DEPTH, CONV_W, D_RNN), CONV_W ** -0.5),
        'conv_b': nrm((DEPTH, D_RNN), 0.01),
        'lru_wa': nrm((DEPTH, N_LRU_BLOCKS, LRU_BLOCK, LRU_BLOCK), LRU_BLOCK ** -0.5),
        'lru_ba': nrm((DEPTH, D_RNN), 0.01),
        'lru_wx': nrm((DEPTH, N_LRU_BLOCKS, LRU_BLOCK, LRU_BLOCK), LRU_BLOCK ** -0.5),
        'lru_bx': nrm((DEPTH, D_RNN), 0.01),
        'lru_lambda': lru_lambda,
        'lambda_q1': nrm((DEPTH, HEAD_DIM), 0.1),
        'lambda_k1': nrm((DEPTH, HEAD_DIM), 0.1),
        'lambda_q2': nrm((DEPTH, HEAD_DIM), 0.1),
        'lambda_k2': nrm((DEPTH, HEAD_DIM), 0.1),
        'subln_g': 1.0 + nrm((DEPTH, V_DIM), 0.02),
        'w_out': nrm((DEPTH, D_MIX, D_MODEL), D_MIX ** -0.5),
        'w_mod': nrm((DEPTH, D_MODEL, 6 * D_MODEL), 0.5 * D_MODEL ** -0.5),
        'b_mod': nrm((DEPTH, 6 * D_MODEL), 0.01),
        'g_pre_mix': 1.0 + nrm((DEPTH, D_MODEL), 0.02),
        'g_post_mix': 1.0 + nrm((DEPTH, D_MODEL), 0.02),
        'g_pre_ffn': 1.0 + nrm((DEPTH, D_MODEL), 0.02),
        'g_post_ffn': 1.0 + nrm((DEPTH, D_MODEL), 0.02),
        'w_gate': nrm((N_DENSE, D_MODEL, D_FF), D_MODEL ** -0.5),
        'w_up': nrm((N_DENSE, D_MODEL, D_FF), D_MODEL ** -0.5),
        'w_down': nrm((N_DENSE, D_FF, D_MODEL), D_FF ** -0.5),
        'router_w': nrm((N_MOE, D_MODEL, N_EXPERTS), D_MODEL ** -0.5),
        'router_b': nrm((N_MOE, N_EXPERTS), 0.01),
        'we_gate': nrm((N_MOE, N_EXPERTS, D_MODEL, D_FF_EXPERT), D_MODEL ** -0.5),
        'we_up': nrm((N_MOE, N_EXPERTS, D_MODEL, D_FF_EXPERT), D_MODEL ** -0.5),
        'we_down': nrm((N_MOE, N_EXPERTS, D_FF_EXPERT, D_MODEL), D_FF_EXPERT ** -0.5),
    }


def reference(x_prompt, x_sample, c_prompt, c_sample, cache_k, cache_v, page_table, state_h, state_conv,
              w_in, conv_w, conv_b, lru_wa, lru_ba, lru_wx, lru_bx, lru_lambda,
              lambda_q1, lambda_k1, lambda_q2, lambda_k2, subln_g, w_out, w_mod, b_mod,
              g_pre_mix, g_post_mix, g_pre_ffn, g_post_ffn, w_gate, w_up, w_down,
              router_w, router_b, we_gate, we_up, we_down):
    p = dict(w_in=w_in, conv_w=conv_w, conv_b=conv_b, lru_wa=lru_wa, lru_ba=lru_ba, lru_wx=lru_wx,
             lru_bx=lru_bx, lru_lambda=lru_lambda, lambda_q1=lambda_q1, lambda_k1=lambda_k1,
             lambda_q2=lambda_q2, lambda_k2=lambda_k2, subln_g=subln_g, w_out=w_out, w_mod=w_mod,
             b_mod=b_mod, g_pre_mix=g_pre_mix, g_post_mix=g_post_mix, g_pre_ffn=g_pre_ffn,
             g_post_ffn=g_post_ffn, w_gate=w_gate, w_up=w_up, w_down=w_down, router_w=router_w,
             router_b=router_b, we_gate=we_gate, we_up=we_up, we_down=we_down)
    bp, sp = x_prompt.shape[0], x_prompt.shape[1]
    pos_p = jnp.arange(sp, dtype=jnp.int32)
    conv0 = jnp.zeros((DEPTH, bp, CONV_W - 1, D_RNN), x_prompt.dtype)
    h0 = jnp.zeros((DEPTH, bp, D_RNN), x_prompt.dtype)
    y_prompt, k_prompt, v_prompt, h_prompt, conv_prompt = _trunk(x_prompt, c_prompt, pos_p, conv0, h0, None, p)
    bs, ss = x_sample.shape[0], x_sample.shape[1]
    past = page_table.shape[1] * cache_k.shape[2]
    pos_s = past + jnp.arange(ss, dtype=jnp.int32)

    def gather_past(l):
        kp = cache_k[l, page_table].reshape(bs, past, N_HEADS, 2, HEAD_DIM)
        vp = cache_v[l, page_table].reshape(bs, past, N_HEADS, V_DIM)
        return kp, vp

    y_sample, k_sample, v_sample, h_sample, conv_sample = _trunk(x_sample, c_sample, pos_s, state_conv, state_h, gather_past, p)
    return (y_prompt, y_sample, k_prompt, v_prompt, h_prompt, conv_prompt, k_sample, v_sample, h_sample, conv_sample)
```

```python
import functools
import math

import numpy as np
import jax
import jax.numpy as jnp
from jax import lax
from jax.experimental import pallas as pl
from jax.experimental.pallas import tpu as pltpu

F32 = jnp.float32
BF16 = jnp.bfloat16
I32 = jnp.int32

D_MODEL = 4096
BATCH = 4
SEQ = 2048
DEPTH = 2
DEC_BATCH = 8
DEC_SEQ = 8
PAGE_SIZE = 128
D_RNN = D_MODEL // 2
N_LRU_BLOCKS = 16
LRU_BLOCK = D_RNN // N_LRU_BLOCKS
CONV_W = 4
LRU_C = 8.0
D_ATT = D_MODEL - D_RNN
HEAD_DIM = 128
N_HEADS = D_ATT // (2 * HEAD_DIM)
V_DIM = 2 * HEAD_DIM
D_QK = N_HEADS * 2 * HEAD_DIM
D_IN = 2 * D_RNN + 2 * D_QK + N_HEADS * V_DIM
ROPE_THETA = 10000.0
D_FF = (D_MODEL * 7) // 2
N_EXPERTS = 8
TOP_K = 2
EPS = 1e-6
NEG_INF = -1e30
ATT_SCALE = HEAD_DIM ** -0.5

M_PROMPT = BATCH * SEQ
M_SAMPLE = DEC_BATCH * DEC_SEQ
M_REAL = M_PROMPT + M_SAMPLE
TOKEN_TILE = 512
M_PAD = -(-M_REAL // TOKEN_TILE) * TOKEN_TILE
N_MOD_ROWS = 16

LANES = 128
SUBLANES = 8
VMEM_LIMIT_BYTES = 56 * 1024 * 1024


def _cparams(n_grid_dims):
    return pltpu.CompilerParams(
        dimension_semantics=("arbitrary",) * n_grid_dims,
        vmem_limit_bytes=VMEM_LIMIT_BYTES)


def _build_schedule(tile_valid, tile_expert, tile_first, nkc, n_steps):
    n_tiles = tile_valid.shape[0]
    spt = jnp.where(jnp.logical_and(tile_valid > 0, tile_first > 0), nkc, 1).astype(I32)
    ends = jnp.cumsum(spt).astype(I32)
    starts = ends - spt
    total = ends[-1]
    s = jnp.arange(n_steps, dtype=I32)
    s_eff = jnp.minimum(s, total - 1)
    t = jnp.minimum(jnp.searchsorted(ends, s_eff, side="right").astype(I32), n_tiles - 1)
    valid = tile_valid[t] > 0
    first = jnp.logical_and(valid, tile_first[t] > 0)
    chunk = jnp.where(first, s_eff - starts[t], nkc - 1).astype(I32)
    kind = jnp.where(s < total, jnp.where(valid, jnp.where(first, 1, 2), 3), 0).astype(I32)
    return kind, t, tile_expert[t].astype(I32), chunk


def _dense_schedule(n_tiles, nkc, expert=0):
    valid = np.ones((n_tiles,), np.int32)
    first = np.zeros((n_tiles,), np.int32)
    first[0] = 1
    return _build_schedule(jnp.asarray(valid), jnp.full((n_tiles,), expert, I32), jnp.asarray(first),
                           nkc, n_tiles + nkc - 1)


def _silu_mul(g, u):
    return jax.nn.silu(g) * u


def _mm_body(kind_ref, tile_ref, exp_ref, chunk_ref, x_ref, *refs, n_w, nkc, kc, swiglu, has_bias,
             x_silu):
    w_refs = refs[:n_w]
    pos = n_w
    bias_ref = None
    if has_bias:
        bias_ref = refs[pos]
        pos += 1
    out_ref = refs[pos]
    wb_refs = refs[pos + 1:pos + 1 + n_w]
    acc_refs = refs[pos + 1 + n_w:pos + 1 + 2 * n_w]
    s = pl.program_id(1)
    kind = kind_ref[s]
    chunk = chunk_ref[s]

    def load_x(lo, hi):
        xs = x_ref[:, lo:hi]
        if x_silu:
            xs = jax.nn.silu(xs).astype(BF16)
        return xs

    def epilogue(vals):
        r = _silu_mul(vals[0], vals[1]) if swiglu else vals[0]
        if has_bias:
            r = r + bias_ref[...]
        return r.astype(out_ref.dtype)

    for ci in range(nkc):
        @pl.when(jnp.logical_and(kind == 1, chunk == ci))
        def _(ci=ci):
            lo = ci * kc
            xs = load_x(lo, lo + kc)
            for w_ref, wb_ref, acc_ref in zip(w_refs, wb_refs, acc_refs):
                wc = w_ref[...].astype(BF16)
                wb_ref[lo:lo + kc, :] = wc
                part = jnp.dot(xs, wc, preferred_element_type=F32)
                if ci == 0:
                    acc_ref[...] = part
                else:
                    acc_ref[...] += part
            if ci == nkc - 1:
                out_ref[...] = epilogue([a[...] for a in acc_refs])

    @pl.when(kind == 2)
    def _():
        xs = load_x(0, nkc * kc)
        out_ref[...] = epilogue([jnp.dot(xs, wb[...], preferred_element_type=F32) for wb in wb_refs])

    @pl.when(kind == 3)
    def _():
        out_ref[...] = jnp.zeros(out_ref.shape, out_ref.dtype)


def _matmul(x, ws, sched, *, tm, tn, kc, out_dtype, swiglu=False, bias=None, x_silu=False, name):
    n_rows, k_dim = x.shape
    n_cols = ws[0].shape[2]
    n_w = len(ws)
    nkc = k_dim // kc
    assert nkc * kc == k_dim and n_cols % tn == 0 and n_rows % tm == 0
    n_steps = sched[0].shape[0]
    imap_x = lambda j, s, kind, tile, exp, chunk: (tile[s], 0)
    imap_w = lambda j, s, kind, tile, exp, chunk: (exp[s], chunk[s], j)
    imap_o = lambda j, s, kind, tile, exp, chunk: (tile[s], j)
    in_specs = [pl.BlockSpec((tm, k_dim), imap_x)]
    in_specs += [pl.BlockSpec((None, kc, tn), imap_w) for _ in ws]
    args = [x] + list(ws)
    if bias is not None:
        in_specs.append(pl.BlockSpec((1, tn), lambda j, s, kind, tile, exp, chunk: (0, j)))
        args.append(bias.reshape(1, n_cols))
    body = functools.partial(_mm_body, n_w=n_w, nkc=nkc, kc=kc, swiglu=swiglu,
                             has_bias=bias is not None, x_silu=x_silu)
    return pl.pallas_call(
        body,
        grid_spec=pltpu.PrefetchScalarGridSpec(
            num_scalar_prefetch=4,
            grid=(n_cols // tn, n_steps),
            in_specs=in_specs,
            out_specs=pl.BlockSpec((tm, tn), imap_o),
            scratch_shapes=[pltpu.VMEM((k_dim, tn), BF16) for _ in ws]
            + [pltpu.VMEM((tm, tn), F32) for _ in ws]),
        out_shape=jax.ShapeDtypeStruct((n_rows, n_cols), out_dtype),
        compiler_params=_cparams(2),
        name=name,
    )(*sched, *args)


NORM_TILE = 128


def _row_mod_onehot(tile_rows):
    row = pl.program_id(0) * tile_rows + lax.broadcasted_iota(I32, (tile_rows, 1), 0)
    prompt_id = jnp.right_shift(row, int(math.log2(SEQ)))
    sample_id = BATCH + jnp.right_shift(row - M_PROMPT, int(math.log2(DEC_SEQ)))
    bid = jnp.minimum(jnp.where(row < M_PROMPT, prompt_id, sample_id), N_MOD_ROWS - 1)
    return (bid == lax.broadcasted_iota(I32, (1, N_MOD_ROWS), 1)).astype(F32)


def _rms(v):
    return v * lax.rsqrt(jnp.mean(v * v, axis=-1, keepdims=True) + EPS)


def _select_rows(onehot, mod_ref):
    return jnp.dot(onehot, mod_ref[...], preferred_element_type=F32, precision=lax.Precision.HIGHEST)


def _norm_body(*refs, has_f, has_next, router):
    refs = list(refs)
    x_ref = refs.pop(0)
    if has_f:
        f_ref, gpost_ref, mgate_ref = refs.pop(0), refs.pop(0), refs.pop(0)
    if has_next:
        gnext_ref, mshift_ref, mscale_ref = refs.pop(0), refs.pop(0), refs.pop(0)
    if router:
        rwt_ref, rb_ref = refs.pop(0), refs.pop(0)
    if has_f:
        xnew_ref = refs.pop(0)
    if has_next:
        h_ref = refs.pop(0)
    if router:
        h32_ref, idx_ref, gate_ref = refs.pop(0), refs.pop(0), refs.pop(0)

    onehot = _row_mod_onehot(NORM_TILE)
    x = x_ref[...]
    if has_f:
        fn = _rms(f_ref[...]) * gpost_ref[...]
        x = x + _select_rows(onehot, mgate_ref) * fn
        xnew_ref[...] = x
    if has_next:
        y = _rms(x) * gnext_ref[...]
        h = y * (1.0 + _select_rows(onehot, mscale_ref)) + _select_rows(onehot, mshift_ref)
        h_ref[...] = h.astype(BF16)
    if router:
        h32_ref[...] = h
        logits = lax.dot_general(rwt_ref[...], h, (((1,), (1,)), ((), ())),
                                 preferred_element_type=F32,
                                 precision=lax.Precision.HIGHEST) + rb_ref[...]
        eidx = lax.broadcasted_iota(I32, logits.shape, 0)
        m1 = jnp.max(logits, axis=0, keepdims=True)
        i1 = jnp.min(jnp.where(logits == m1, eidx, N_EXPERTS), axis=0, keepdims=True)
        rest = jnp.where(eidx == i1, -jnp.inf, logits)
        m2 = jnp.max(rest, axis=0, keepdims=True)
        i2 = jnp.min(jnp.where(rest == m2, eidx, N_EXPERTS), axis=0, keepdims=True)
        e2 = jnp.exp(m2 - m1)
        denom = 1.0 + e2
        idx_ref[...] = jnp.concatenate([i1, i2], axis=0)
        gate_ref[...] = jnp.concatenate([1.0 / denom, e2 / denom], axis=0)


def _norm(x, mod, *, f=None, g_post=None, gate_col=None, g_next=None, shift_col=None, scale_col=None,
          router_w=None, router_b=None, mod_next=None, name):
    if mod_next is None:
        mod_next = mod
    has_f, has_next, router = f is not None, g_next is not None, router_w is not None
    row_spec = pl.BlockSpec((NORM_TILE, D_MODEL), lambda i: (i, 0))
    vec_spec = pl.BlockSpec((1, D_MODEL), lambda i: (0, 0))
    mod_spec = lambda col: pl.BlockSpec((N_MOD_ROWS, D_MODEL), lambda i, col=col: (0, col))
    args, in_specs, out_shape, out_specs = [x], [row_spec], [], []
    if has_f:
        args += [f, g_post.reshape(1, D_MODEL), mod]
        in_specs += [row_spec, vec_spec, mod_spec(gate_col)]
        out_shape.append(jax.ShapeDtypeStruct((M_PAD, D_MODEL), F32))
        out_specs.append(row_spec)
    if has_next:
        args += [g_next.reshape(1, D_MODEL), mod_next, mod_next]
        in_specs += [vec_spec, mod_spec(shift_col), mod_spec(scale_col)]
        out_shape.append(jax.ShapeDtypeStruct((M_PAD, D_MODEL), BF16))
        out_specs.append(row_spec)
    if router:
        args += [router_w.T, router_b.reshape(N_EXPERTS, 1)]
        in_specs += [pl.BlockSpec((N_EXPERTS, D_MODEL), lambda i: (0, 0)),
                     pl.BlockSpec((N_EXPERTS, 1), lambda i: (0, 0))]
        out_shape += [jax.ShapeDtypeStruct((M_PAD, D_MODEL), F32),
                      jax.ShapeDtypeStruct((TOP_K, M_PAD), I32),
                      jax.ShapeDtypeStruct((TOP_K, M_PAD), F32)]
        out_specs += [row_spec, pl.BlockSpec((TOP_K, NORM_TILE), lambda i: (0, i)),
                      pl.BlockSpec((TOP_K, NORM_TILE), lambda i: (0, i))]
    return pl.pallas_call(
        functools.partial(_norm_body, has_f=has_f, has_next=has_next, router=router),
        grid=(M_PAD // NORM_TILE,),
        in_specs=in_specs, out_specs=out_specs, out_shape=out_shape,
        compiler_params=_cparams(1), name=name,
    )(*args)


ROPE_TILE = 256


def _rope_body(q_ref, k_ref, v_ref, cos_ref, sin_ref, qb_ref, kf_ref, kb_ref, vb_ref):
    cos = cos_ref[...]
    sin = sin_ref[...]

    def rot(xg):
        return xg * cos + pltpu.roll(xg, HEAD_DIM // 2, axis=1) * sin

    for g in range(D_QK // HEAD_DIM):
        cols = slice(g * HEAD_DIM, (g + 1) * HEAD_DIM)
        qb_ref[:, cols] = rot(q_ref[:, cols]).astype(BF16)
        kr = rot(k_ref[:, cols])
        kf_ref[:, cols] = kr
        kb_ref[:, cols] = kr.astype(BF16)
    vb_ref[...] = v_ref[...].astype(BF16)


def _rope(proj, cos_tab, sin_tab):
    blk = lambda col: pl.BlockSpec((ROPE_TILE, D_QK), lambda i, col=col: (i, col))
    tab = pl.BlockSpec((ROPE_TILE, HEAD_DIM), lambda i: (i, 0))
    q_col = 2 * D_RNN // D_QK
    return pl.pallas_call(
        _rope_body,
        grid=(M_PAD // ROPE_TILE,),
        in_specs=[blk(q_col), blk(q_col + 1), blk(q_col + 2), tab, tab],
        out_specs=[blk(0), blk(0), blk(0), blk(0)],
        out_shape=[jax.ShapeDtypeStruct((M_PAD, D_QK), BF16), jax.ShapeDtypeStruct((M_PAD, D_QK), F32),
                   jax.ShapeDtypeStruct((M_PAD, D_QK), BF16), jax.ShapeDtypeStruct((M_PAD, D_QK), BF16)],
        compiler_params=_cparams(1), name="rope",
    )(proj, proj, proj, cos_tab, sin_tab)


LRU_COLS = 512


def _rglru_body(xr_ref, gr_ref, hist_ref, h0_ref, cw_ref, cb_ref, wa_ref, ba_ref, wx_ref, bx_ref,
                lam_ref, *refs, tc, aliased):
    if aliased:
        refs = refs[1:]
    y_ref, hl_ref, prev_scr, h_scr = refs
    t = pl.program_id(2)

    @pl.when(t == 0)
    def _():
        prev_scr[...] = hist_ref[0]
        h_scr[...] = h0_ref[0]

    x = xr_ref[...]
    xp = jnp.concatenate([prev_scr[...], x], axis=0)
    w = cw_ref[...]
    xc = cb_ref[...] + w[0:1] * pltpu.roll(xp, 3, axis=0)[SUBLANES:]
    xc = xc + w[1:2] * pltpu.roll(xp, 2, axis=0)[SUBLANES:]
    xc = xc + w[2:3] * pltpu.roll(xp, 1, axis=0)[SUBLANES:]
    xc = xc + w[3:4] * x

    ra, gi = [], []
    for n in range(LRU_COLS // LRU_BLOCK):
        xb = xc[:, n * LRU_BLOCK:(n + 1) * LRU_BLOCK].astype(BF16)
        ra.append(jnp.dot(xb, wa_ref[n].astype(BF16), preferred_element_type=F32))
        gi.append(jnp.dot(xb, wx_ref[n].astype(BF16), preferred_element_type=F32))
    r = jax.nn.sigmoid(jnp.concatenate(ra, axis=1) + ba_ref[...])
    gin = jax.nn.sigmoid(jnp.concatenate(gi, axis=1) + bx_ref[...])
    z = -lam_ref[...]
    softplus = jnp.maximum(z, 0.0) + jnp.log1p(jnp.exp(-jnp.abs(z)))
    log_a = -LRU_C * r * softplus
    a_cum = jnp.exp(log_a)
    one_minus_a2 = -jnp.tanh(log_a) * (jnp.exp(2.0 * log_a) + 1.0)
    b_cum = jnp.sqrt(one_minus_a2) * (gin * xc)

    rows = lax.broadcasted_iota(I32, a_cum.shape, 0)
    d = 1
    while d < tc:
        keep = rows >= d
        a_sh = jnp.where(keep, pltpu.roll(a_cum, d, axis=0), 1.0)
        b_sh = jnp.where(keep, pltpu.roll(b_cum, d, axis=0), 0.0)
        b_cum = a_cum * b_sh + b_cum
        a_cum = a_sh * a_cum
        d *= 2
    h = a_cum * h_scr[...] + b_cum
    y_ref[...] = (h * jax.nn.gelu(gr_ref[...])).astype(y_ref.dtype)
    h_scr[...] = h[tc - 1:tc]
    prev_scr[...] = x[tc - SUBLANES:tc]

    @pl.when(t == pl.num_programs(2) - 1)
    def _():
        hl_ref[0] = h[tc - 1:tc]


def _rglru(proj, *, row0, n_batch, seq, tc, hist8, h0, conv_w, conv_b, wa, ba, wx, bx, lam, buf, name):
    n_tc = seq // tc
    rb0 = row0 // tc
    n_cb = D_RNN // LRU_COLS
    nb = LRU_COLS // LRU_BLOCK
    row_blk = lambda off: pl.BlockSpec((tc, LRU_COLS), lambda b, c, t, off=off: (rb0 + b * n_tc + t, off + c))
    vec = lambda: pl.BlockSpec((1, LRU_COLS), lambda b, c, t: (0, c))
    in_specs = [
        row_blk(0), row_blk(n_cb),
        pl.BlockSpec((1, SUBLANES, LRU_COLS), lambda b, c, t: (b, 0, c)),
        pl.BlockSpec((1, 1, LRU_COLS), lambda b, c, t: (b, 0, c)),
        pl.BlockSpec((CONV_W, LRU_COLS), lambda b, c, t: (0, c)), vec(),
        pl.BlockSpec((nb, LRU_BLOCK, LRU_BLOCK), lambda b, c, t: (c, 0, 0)), vec(),
        pl.BlockSpec((nb, LRU_BLOCK, LRU_BLOCK), lambda b, c, t: (c, 0, 0)), vec(), vec()]
    args = [proj, proj, hist8, h0.reshape(n_batch, 1, D_RNN), conv_w, conv_b.reshape(1, D_RNN),
            wa, ba.reshape(1, D_RNN), wx, bx.reshape(1, D_RNN), lam.reshape(1, D_RNN)]
    aliases = {}
    if buf is not None:
        in_specs.append(pl.BlockSpec(memory_space=pl.ANY))
        args.append(buf)
        aliases = {len(args) - 1: 0}
        y_shape = jax.ShapeDtypeStruct(buf.shape, buf.dtype)
        y_spec = pl.BlockSpec((tc, LRU_COLS), lambda b, c, t: (rb0 + b * n_tc + t, c))
    else:
        y_shape = jax.ShapeDtypeStruct((n_batch * seq, D_RNN), F32)
        y_spec = pl.BlockSpec((tc, LRU_COLS), lambda b, c, t: (b * n_tc + t, c))
    return pl.pallas_call(
        functools.partial(_rglru_body, tc=tc, aliased=buf is not None),
        grid=(n_batch, n_cb, n_tc),
        in_specs=in_specs,
        out_specs=[y_spec, pl.BlockSpec((1, 1, LRU_COLS), lambda b, c, t: (b, 0, c))],
        out_shape=[y_shape, jax.ShapeDtypeStruct((n_batch, 1, D_RNN), F32)],
        scratch_shapes=[pltpu.VMEM((SUBLANES, LRU_COLS), F32), pltpu.VMEM((1, LRU_COLS), F32)],
        input_output_aliases=aliases,
        compiler_params=_cparams(3), name=name,
    )(*args)


def _diff_lambda(lamv_ref, lam_init):
    lv = lamv_ref[...]
    s1 = jnp.sum(lv[0:1] * lv[1:2], axis=-1, keepdims=True)
    s2 = jnp.sum(lv[2:3] * lv[3:4], axis=-1, keepdims=True)
    return jnp.exp(s1) - jnp.exp(s2) + lam_init


def _diff_finalize(o0, o1, lam, sg, lam_init):
    o = o0 - lam * o1
    return (_rms(o) * sg) * (1.0 - lam_init)


def _nt_dot(a, b):
    return lax.dot_general(a, b, (((1,), (1,)), ((), ())), preferred_element_type=F32)


ATT_TILE = 512


def _flash_body(q_ref, k_ref, v_ref, lamv_ref, sg_ref, buf_ref, o_ref, m_scr, l_scr, acc_scr, *, lam_init):
    i = pl.program_id(2)
    j = pl.program_id(3)

    @pl.when(j == 0)
    def _():
        m_scr[...] = jnp.full(m_scr.shape, NEG_INF, F32)
        l_scr[...] = jnp.zeros(l_scr.shape, F32)
        acc_scr[...] = jnp.zeros(acc_scr.shape, F32)

    def update(masked):
        v = v_ref[...]
        for c in range(2):
            cols = slice(c * HEAD_DIM, (c + 1) * HEAD_DIM)
            s = _nt_dot(q_ref[:, cols], k_ref[:, cols]) * ATT_SCALE
            if masked:
                row = lax.broadcasted_iota(I32, s.shape, 0)
                col = lax.broadcasted_iota(I32, s.shape, 1)
                s = jnp.where(col <= row, s, NEG_INF)
            m_old = m_scr[c]
            m_new = jnp.maximum(m_old, jnp.max(s, axis=-1, keepdims=True))
            alpha = jnp.exp(m_old - m_new)
            p = jnp.exp(s - m_new)
            l_scr[c] = alpha * l_scr[c] + jnp.sum(p, axis=-1, keepdims=True)
            acc_scr[c] = alpha * acc_scr[c] + jnp.dot(p.astype(BF16), v, preferred_element_type=F32)
            m_scr[c] = m_new

    @pl.when(j < i)
    def _():
        update(False)

    @pl.when(j == i)
    def _():
        update(True)
        lam = _diff_lambda(lamv_ref, lam_init)
        o = _diff_finalize(acc_scr[0] / l_scr[0], acc_scr[1] / l_scr[1], lam, sg_ref[...], lam_init)
        o_ref[...] = o.astype(o_ref.dtype)


def _flash(q_bf, k_bf, v_bf, lamv, subln_g, buf, lam_init):
    n_qt = SEQ // ATT_TILE
    q_spec = pl.BlockSpec((ATT_TILE, V_DIM), lambda b, h, i, j: (b * n_qt + i, h))
    kv_spec = pl.BlockSpec((ATT_TILE, V_DIM), lambda b, h, i, j: (b * n_qt + jnp.minimum(i, j), h))
    return pl.pallas_call(
        functools.partial(_flash_body, lam_init=lam_init),
        grid=(BATCH, N_HEADS, n_qt, n_qt),
        in_specs=[q_spec, kv_spec, kv_spec,
                  pl.BlockSpec((4, HEAD_DIM), lambda b, h, i, j: (0, 0)),
                  pl.BlockSpec((1, V_DIM), lambda b, h, i, j: (0, 0)),
                  pl.BlockSpec(memory_space=pl.ANY)],
        out_specs=pl.BlockSpec((ATT_TILE, V_DIM), lambda b, h, i, j: (b * n_qt + i, D_RNN // V_DIM + h)),
        out_shape=jax.ShapeDtypeStruct(buf.shape, buf.dtype),
        scratch_shapes=[pltpu.VMEM((2, ATT_TILE, 1), F32), pltpu.VMEM((2, ATT_TILE, 1), F32),
                        pltpu.VMEM((2, ATT_TILE, V_DIM), F32)],
        input_output_aliases={5: 0},
        compiler_params=_cparams(4), name="flash_prompt",
    )(q_bf, k_bf, v_bf, lamv, subln_g.reshape(1, V_DIM), buf)


def _paged_body(pt_ref, q_ref, k_ref, v_ref, kn_ref, vn_ref, lamv_ref, sg_ref, o_ref,
                m_scr, l_scr, acc_scr, *, lam_init, n_pages):
    p = pl.program_id(1)

    @pl.when(p == 0)
    def _():
        m_scr[...] = jnp.full(m_scr.shape, NEG_INF, F32)
        l_scr[...] = jnp.zeros(l_scr.shape, F32)
        acc_scr[...] = jnp.zeros(acc_scr.shape, F32)

    q = q_ref[0].astype(BF16)

    def process(get_k, get_v, mask):
        for h in range(N_HEADS):
            v_h = get_v(h).astype(BF16)
            probs = []
            for c in range(2):
                hc = 2 * h + c
                cols = slice(hc * HEAD_DIM, (hc + 1) * HEAD_DIM)
                s = _nt_dot(q[:, cols], get_k(h, c).astype(BF16)) * ATT_SCALE
                if mask is not None:
                    s = jnp.where(mask, s, NEG_INF)
                m_old = m_scr[hc]
                m_new = jnp.maximum(m_old, jnp.max(s, axis=-1, keepdims=True))
                alpha = jnp.exp(m_old - m_new)
                pr = jnp.exp(s - m_new)
                l_scr[hc] = alpha * l_scr[hc] + jnp.sum(pr, axis=-1, keepdims=True)
                acc_scr[hc] = alpha * acc_scr[hc]
                m_scr[hc] = m_new
                probs.append(pr)
            pv = jnp.dot(jnp.concatenate(probs, axis=0).astype(BF16), v_h, preferred_element_type=F32)
            acc_scr[2 * h] += pv[:DEC_SEQ]
            acc_scr[2 * h + 1] += pv[DEC_SEQ:]

    process(lambda h, c: k_ref[:, h, c, :], lambda h: v_ref[:, h, :], None)

    @pl.when(p == n_pages - 1)
    def _():
        row = lax.broadcasted_iota(I32, (DEC_SEQ, PAGE_SIZE), 0)
        col = lax.broadcasted_iota(I32, (DEC_SEQ, PAGE_SIZE), 1)
        process(lambda h, c: kn_ref[0, :, (2 * h + c) * HEAD_DIM:(2 * h + c + 1) * HEAD_DIM],
                lambda h: vn_ref[0, :, h * V_DIM:(h + 1) * V_DIM], col <= row)
        lam = _diff_lambda(lamv_ref, lam_init)
        sg = sg_ref[...]
        for h in range(N_HEADS):
            o = _diff_finalize(acc_scr[2 * h] / l_scr[2 * h], acc_scr[2 * h + 1] / l_scr[2 * h + 1],
                               lam, sg, lam_init)
            o_ref[0, :, h * V_DIM:(h + 1) * V_DIM] = o


def _paged(q_s, cache_k, cache_v, layer, page_table, k_new, v_new, lamv, subln_g, lam_init):
    n_pages = page_table.shape[1]
    kpage_spec = pl.BlockSpec((None, None, PAGE_SIZE, N_HEADS, 2, HEAD_DIM),
                              lambda b, p, pt: (layer, pt[b * n_pages + p], 0, 0, 0, 0))
    vpage_spec = pl.BlockSpec((None, None, PAGE_SIZE, N_HEADS, V_DIM),
                              lambda b, p, pt: (layer, pt[b * n_pages + p], 0, 0, 0))
    new_spec = pl.BlockSpec((1, PAGE_SIZE, D_QK), lambda b, p, pt: (b, 0, 0))
    q_spec = pl.BlockSpec((1, DEC_SEQ, D_QK), lambda b, p, pt: (b, 0, 0))
    return pl.pallas_call(
        functools.partial(_paged_body, lam_init=lam_init, n_pages=n_pages),
        grid_spec=pltpu.PrefetchScalarGridSpec(
            num_scalar_prefetch=1,
            grid=(DEC_BATCH, n_pages),
            in_specs=[q_spec, kpage_spec, vpage_spec, new_spec, new_spec,
                      pl.BlockSpec((4, HEAD_DIM), lambda b, p, pt: (0, 0)),
                      pl.BlockSpec((1, V_DIM), lambda b, p, pt: (0, 0))],
            out_specs=pl.BlockSpec((1, DEC_SEQ, D_ATT), lambda b, p, pt: (b, 0, 0)),
            scratch_shapes=[pltpu.VMEM((2 * N_HEADS, DEC_SEQ, 1), F32),
                            pltpu.VMEM((2 * N_HEADS, DEC_SEQ, 1), F32),
                            pltpu.VMEM((2 * N_HEADS, DEC_SEQ, V_DIM), F32)]),
        out_shape=jax.ShapeDtypeStruct((DEC_BATCH, DEC_SEQ, D_ATT), F32),
        compiler_params=_cparams(2), name="paged_sample",
    )(page_table.reshape(-1), q_s, cache_k, cache_v, k_new, v_new, lamv, subln_g.reshape(1, V_DIM))


N_ASSIGN = TOP_K * M_REAL
MOE_TILES = (N_ASSIGN + N_EXPERTS * (TOKEN_TILE - 1)) // TOKEN_TILE
MOE_ROWS = MOE_TILES * TOKEN_TILE
GATHER_TILE = 256
COMBINE_TILE = 128


def _row_copy(src_hbm, row, dst_vmem, slot, sem):
    return pltpu.make_async_copy(src_hbm.at[pl.ds(row, 1), :], dst_vmem.at[pl.ds(slot, 1), :], sem)


def _gather_body(src_ref, x_hbm, o_ref, buf, sem):
    def start(r, carry):
        _row_copy(x_hbm, src_ref[0, r], buf, r, sem).start()
        return carry

    def wait(r, carry):
        _row_copy(x_hbm, src_ref[0, r], buf, r, sem).wait()
        return carry

    lax.fori_loop(0, GATHER_TILE, start, 0)
    lax.fori_loop(0, GATHER_TILE, wait, 0)
    o_ref[...] = buf[...].astype(BF16)


def _gather_rows(src_rows, x32):
    n_t = MOE_ROWS // GATHER_TILE
    return pl.pallas_call(
        _gather_body,
        grid=(n_t,),
        in_specs=[pl.BlockSpec((None, 1, GATHER_TILE), lambda i: (i, 0, 0), memory_space=pltpu.SMEM),
                  pl.BlockSpec(memory_space=pl.ANY)],
        out_specs=pl.BlockSpec((GATHER_TILE, D_MODEL), lambda i: (i, 0)),
        out_shape=jax.ShapeDtypeStruct((MOE_ROWS, D_MODEL), BF16),
        scratch_shapes=[pltpu.VMEM((GATHER_TILE, D_MODEL), F32), pltpu.SemaphoreType.DMA(())],
        compiler_params=_cparams(1), name="moe_gather",
    )(src_rows.reshape(n_t, 1, GATHER_TILE), x32)


def _combine_body(pos_ref, ys_hbm, g_ref, o_ref, buf, sem):
    def start(r, carry):
        for k in range(TOP_K):
            _row_copy(ys_hbm, pos_ref[k, r], buf.at[k], r, sem).start()
        return carry

    def wait(r, carry):
        for k in range(TOP_K):
            _row_copy(ys_hbm, pos_ref[k, r], buf.at[k], r, sem).wait()
        return carry

    lax.fori_loop(0, COMBINE_TILE, start, 0)
    lax.fori_loop(0, COMBINE_TILE, wait, 0)
    g = g_ref[...]
    o_ref[...] = g[:, 0:1] * buf[0] + g[:, 1:2] * buf[1]


def _combine_rows(pos, gates_t, ys):
    n_t = M_PAD // COMBINE_TILE
    pos_blk = pos.reshape(TOP_K, n_t, COMBINE_TILE).transpose(1, 0, 2)
    return pl.pallas_call(
        _combine_body,
        grid=(n_t,),
        in_specs=[pl.BlockSpec((None, TOP_K, COMBINE_TILE), lambda i: (i, 0, 0), memory_space=pltpu.SMEM),
                  pl.BlockSpec(memory_space=pl.ANY),
                  pl.BlockSpec((COMBINE_TILE, TOP_K), lambda i: (i, 0))],
        out_specs=pl.BlockSpec((COMBINE_TILE, D_MODEL), lambda i: (i, 0)),
        out_shape=jax.ShapeDtypeStruct((M_PAD, D_MODEL), F32),
        scratch_shapes=[pltpu.VMEM((TOP_K, COMBINE_TILE, D_MODEL), F32), pltpu.SemaphoreType.DMA(())],
        compiler_params=_cparams(1), name="moe_combine",
    )(pos_blk, ys, gates_t)


def _moe_plan(idx):
    e_a = idx[:, :M_REAL].reshape(-1)
    counts = jnp.sum((e_a[:, None] == jnp.arange(N_EXPERTS, dtype=I32)[None, :]).astype(I32), axis=0)
    tiles_e = (counts + TOKEN_TILE - 1) // TOKEN_TILE
    tile_end = jnp.cumsum(tiles_e).astype(I32)
    tile_start = tile_end - tiles_e
    order = jnp.argsort(e_a, stable=True).astype(I32)
    sorted_pos = jnp.zeros((N_ASSIGN,), I32).at[order].set(jnp.arange(N_ASSIGN, dtype=I32))
    first_sorted = (jnp.cumsum(counts) - counts).astype(I32)
    dest = tile_start[e_a] * TOKEN_TILE + sorted_pos - first_sorted[e_a]
    token = jnp.tile(jnp.arange(M_REAL, dtype=I32), TOP_K)
    src_rows = jnp.zeros((MOE_ROWS,), I32).at[dest].set(token)
    pos = jnp.zeros((TOP_K, M_PAD), I32).at[:, :M_REAL].set(dest.reshape(TOP_K, M_REAL))
    tl = jnp.arange(MOE_TILES, dtype=I32)
    tl_used = jnp.minimum(tl, tile_end[-1] - 1)
    tile_expert = jnp.minimum(jnp.searchsorted(tile_end, tl_used, side="right").astype(I32), N_EXPERTS - 1)
    tile_valid = (tl < tile_end[-1]).astype(I32)
    tile_first = jnp.logical_and(tile_valid > 0, tl == tile_start[tile_expert]).astype(I32)
    return src_rows, pos, tile_valid, tile_expert, tile_first


PROJ_TN, PROJ_KC = 1024, 1024
GU_TN, GU_KC = 512, 1024
DOWN_TM, DOWN_TN, DOWN_KC = 256, 512, 2048


def _ffn(h_rows, wg, wu, wd, sched_gu, sched_down, tag):
    hmid = _matmul(h_rows, [wg, wu], sched_gu, tm=TOKEN_TILE, tn=GU_TN, kc=GU_KC, out_dtype=BF16,
                   swiglu=True, name=tag + "_gate_up")
    return _matmul(hmid, [wd], sched_down, tm=DOWN_TM, tn=DOWN_TN, kc=DOWN_KC, out_dtype=F32,
                   name=tag + "_down")


def _split_tiles(valid, expert, first, parts):
    sub_first = jnp.stack([first] + [jnp.zeros_like(first)] * (parts - 1), axis=1).reshape(-1)
    return jnp.repeat(valid, parts), jnp.repeat(expert, parts), sub_first


def kernel(x_prompt, x_sample, c_prompt, c_sample, cache_k, cache_v, page_table, state_h, state_conv, w_in, conv_w, conv_b, lru_wa, lru_ba, lru_wx, lru_bx, lru_lambda, lambda_q1, lambda_k1, lambda_q2, lambda_k2, subln_g, w_out, w_mod, b_mod, g_pre_mix, g_post_mix, g_pre_ffn, g_post_ffn, w_gate, w_up, w_down, router_w, router_b, we_gate, we_up, we_down):
    n_pages = page_table.shape[1]
    past_len = n_pages * cache_k.shape[2]

    x = jnp.concatenate([x_prompt.reshape(M_PROMPT, D_MODEL), x_sample.reshape(M_SAMPLE, D_MODEL),
                         jnp.zeros((M_PAD - M_REAL, D_MODEL), F32)], axis=0)
    c_rows = jnp.concatenate([c_prompt, c_sample,
                              jnp.zeros((N_MOD_ROWS - BATCH - DEC_BATCH, D_MODEL), F32)], axis=0)

    pos_rows = np.concatenate([np.tile(np.arange(SEQ), BATCH),
                               np.tile(past_len + np.arange(DEC_SEQ), DEC_BATCH),
                               np.zeros((M_PAD - M_REAL,))]).astype(np.float32)
    half = HEAD_DIM // 2
    inv_freq = ROPE_THETA ** (-jnp.arange(half, dtype=F32) / half)
    ang = jnp.asarray(pos_rows)[:, None] * inv_freq[None, :]
    cos_tab = jnp.concatenate([jnp.cos(ang), jnp.cos(ang)], axis=1)
    sin_tab = jnp.concatenate([-jnp.sin(ang), jnp.sin(ang)], axis=1)

    n_tok_tiles = M_PAD // TOKEN_TILE
    sched_k4096 = [_dense_schedule(n_tok_tiles, D_MODEL // PROJ_KC, l) for l in range(DEPTH)]
    sched_gu = [_dense_schedule(n_tok_tiles, D_MODEL // GU_KC, i) for i in range(w_gate.shape[0])]
    sched_down = [_dense_schedule(M_PAD // DOWN_TM, D_FF // DOWN_KC, i) for i in range(w_down.shape[0])]
    sched_mod = [_dense_schedule(1, D_MODEL // PROJ_KC, l) for l in range(DEPTH)]

    mods = [_matmul(c_rows, [w_mod], sched_mod[l], tm=N_MOD_ROWS, tn=PROJ_TN, kc=PROJ_KC,
                    out_dtype=F32, bias=b_mod[l], x_silu=True, name=f"mod{l}") for l in range(DEPTH)]
    moe_w = [w.reshape((-1,) + w.shape[2:]) for w in (we_gate, we_up, we_down)]

    hm = _norm(x, mods[0], g_next=g_pre_mix[0], shift_col=0, scale_col=1, name="pre_mix0")[0]

    zeros_hist = jnp.zeros((BATCH, SUBLANES, D_RNN), F32)
    zeros_h0 = jnp.zeros((BATCH, D_RNN), F32)
    pad_new = lambda a: jnp.pad(a.reshape(DEC_BATCH, DEC_SEQ, -1), ((0, 0), (0, PAGE_SIZE - DEC_SEQ), (0, 0)))

    ks, vs, hs, cs = [], [], [], []
    for l in range(DEPTH):
        mod = mods[l]
        lam_init = 0.8 - 0.6 * math.exp(-0.3 * l)
        proj = _matmul(hm, [w_in], sched_k4096[l], tm=TOKEN_TILE, tn=PROJ_TN, kc=PROJ_KC,
                       out_dtype=F32, name=f"w_in{l}")
        q_bf, k_f32, k_bf, v_bf = _rope(proj, cos_tab, sin_tab)

        lru = dict(conv_w=conv_w[l], conv_b=conv_b[l], wa=lru_wa[l], ba=lru_ba[l], wx=lru_wx[l],
                   bx=lru_bx[l], lam=lru_lambda[l])
        buf = jnp.zeros((M_PAD, D_MODEL), BF16)
        buf, h_last_p = _rglru(proj, row0=0, n_batch=BATCH, seq=SEQ, tc=256, hist8=zeros_hist,
                               h0=zeros_h0, buf=buf, name=f"rglru_prompt{l}", **lru)
        hist_s = jnp.pad(state_conv[l], ((0, 0), (SUBLANES - (CONV_W - 1), 0), (0, 0)))
        y_s, h_last_s = _rglru(proj, row0=M_PROMPT, n_batch=DEC_BATCH, seq=DEC_SEQ, tc=DEC_SEQ,
                               hist8=hist_s, h0=state_h[l], buf=None, name=f"rglru_sample{l}", **lru)

        lamv = jnp.stack([lambda_q1[l], lambda_k1[l], lambda_q2[l], lambda_k2[l]], axis=0)
        buf = _flash(q_bf, k_bf, v_bf, lamv, subln_g[l], buf, lam_init)
        v_s = proj[M_PROMPT:M_REAL, 2 * D_RNN + 2 * D_QK:]
        o_s = _paged(q_bf[M_PROMPT:M_REAL].astype(F32).reshape(DEC_BATCH, DEC_SEQ, D_QK),
                     cache_k, cache_v, l, page_table, pad_new(k_f32[M_PROMPT:M_REAL]), pad_new(v_s),
                     lamv, subln_g[l], lam_init)
        mix_s = jnp.concatenate([y_s, o_s.reshape(M_SAMPLE, D_ATT)], axis=1).astype(BF16)
        buf = lax.dynamic_update_slice(buf, mix_s, (M_PROMPT, 0))

        m = _matmul(buf, [w_out], sched_k4096[l], tm=TOKEN_TILE, tn=PROJ_TN, kc=PROJ_KC,
                    out_dtype=F32, name=f"w_out{l}")
        i = l // 2
        if l % 2 == 0:
            x, hf = _norm(x, mod, f=m, g_post=g_post_mix[l], gate_col=2, g_next=g_pre_ffn[l],
                          shift_col=3, scale_col=4, name=f"post_mix{l}")
            f = _ffn(hf, w_gate, w_up, w_down, sched_gu[i], sched_down[i], f"ffn{l}")
        else:
            x, hf, hf32, idx, gates = _norm(x, mod, f=m, g_post=g_post_mix[l], gate_col=2,
                                            g_next=g_pre_ffn[l], shift_col=3, scale_col=4,
                                            router_w=router_w[i], router_b=router_b[i],
                                            name=f"post_mix{l}")
            src_rows, pos, t_valid, t_expert, t_first = _moe_plan(idx)
            t_expert = t_expert + i * N_EXPERTS
            xs = _gather_rows(src_rows, hf32)
            nkc_gu, nkc_dn = D_MODEL // GU_KC, D_FF // DOWN_KC
            moe_gu = _build_schedule(t_valid, t_expert, t_first, nkc_gu,
                                     MOE_TILES + N_EXPERTS * (nkc_gu - 1))
            parts = TOKEN_TILE // DOWN_TM
            moe_dn = _build_schedule(*_split_tiles(t_valid, t_expert, t_first, parts), nkc_dn,
                                     MOE_TILES * parts + N_EXPERTS * (nkc_dn - 1))
            ys = _ffn(xs, moe_w[0], moe_w[1], moe_w[2], moe_gu, moe_dn, f"moe{l}")
            gates_t = jnp.where(jnp.arange(M_PAD)[:, None] < M_REAL, gates.T, 0.0)
            f = _combine_rows(pos, gates_t, ys)
        if l + 1 < DEPTH:
            x, hm = _norm(x, mod, f=f, g_post=g_post_ffn[l], gate_col=5, g_next=g_pre_mix[l + 1],
                          shift_col=0, scale_col=1, mod_next=mods[l + 1], name=f"post_ffn{l}")
        else:
            x = _norm(x, mod, f=f, g_post=g_post_ffn[l], gate_col=5, name=f"post_ffn{l}")[0]

        ks.append(k_f32)
        vs.append(proj[:, 2 * D_RNN + 2 * D_QK:])
        hs.append((h_last_p, h_last_s))
        cs.append(proj[:, :D_RNN])

    def prompt_rows(a):
        return a[:M_PROMPT]

    def sample_rows(a):
        return a[M_PROMPT:M_REAL]

    y_prompt = prompt_rows(x).reshape(BATCH, SEQ, D_MODEL)
    y_sample = sample_rows(x).reshape(DEC_BATCH, DEC_SEQ, D_MODEL)
    k_prompt = jnp.stack([prompt_rows(k).reshape(BATCH, SEQ, N_HEADS, 2, HEAD_DIM) for k in ks])
    k_sample = jnp.stack([sample_rows(k).reshape(DEC_BATCH, DEC_SEQ, N_HEADS, 2, HEAD_DIM) for k in ks])
    v_prompt = jnp.stack([prompt_rows(v).reshape(BATCH, SEQ, N_HEADS, V_DIM) for v in vs])
    v_sample = jnp.stack([sample_rows(v).reshape(DEC_BATCH, DEC_SEQ, N_HEADS, V_DIM) for v in vs])
    h_prompt = jnp.stack([h[0].reshape(BATCH, D_RNN) for h in hs])
    h_sample = jnp.stack([h[1].reshape(DEC_BATCH, D_RNN) for h in hs])
    conv_prompt = jnp.stack([prompt_rows(c).reshape(BATCH, SEQ, D_RNN)[:, SEQ - (CONV_W - 1):] for c in cs])
    conv_sample = jnp.stack([sample_rows(c).reshape(DEC_BATCH, DEC_SEQ, D_RNN)[:, DEC_SEQ - (CONV_W - 1):]
                             for c in cs])
    return (y_prompt, y_sample, k_prompt, v_prompt, h_prompt, conv_prompt,
            k_sample, v_sample, h_sample, conv_sample)
```

```python
import functools
import math

import numpy as np
import jax
import jax.numpy as jnp
from jax import lax
from jax.experimental import pallas as pl
from jax.experimental.pallas import tpu as pltpu

F32 = jnp.float32
BF16 = jnp.bfloat16
I32 = jnp.int32

D_MODEL = 4096
BATCH = 4
SEQ = 2048
DEPTH = 2
DEC_BATCH = 8
DEC_SEQ = 8
PAGE_SIZE = 128
D_RNN = D_MODEL // 2
N_LRU_BLOCKS = 16
LRU_BLOCK = D_RNN // N_LRU_BLOCKS
CONV_W = 4
LRU_C = 8.0
D_ATT = D_MODEL - D_RNN
HEAD_DIM = 128
N_HEADS = D_ATT // (2 * HEAD_DIM)
V_DIM = 2 * HEAD_DIM
D_QK = N_HEADS * 2 * HEAD_DIM
D_IN = 2 * D_RNN + 2 * D_QK + N_HEADS * V_DIM
ROPE_THETA = 10000.0
D_FF = (D_MODEL * 7) // 2
N_EXPERTS = 8
TOP_K = 2
EPS = 1e-6
NEG_INF = -1e30
ATT_SCALE = HEAD_DIM ** -0.5

M_PROMPT = BATCH * SEQ
M_SAMPLE = DEC_BATCH * DEC_SEQ
M_REAL = M_PROMPT + M_SAMPLE
TOKEN_TILE = 512
M_PAD = -(-M_REAL // TOKEN_TILE) * TOKEN_TILE
N_MOD_ROWS = 16

LANES = 128
SUBLANES = 8
VMEM_LIMIT_BYTES = 56 * 1024 * 1024


def _cparams(n_grid_dims):
    return pltpu.CompilerParams(
        dimension_semantics=("arbitrary",) * n_grid_dims,
        vmem_limit_bytes=VMEM_LIMIT_BYTES)


def _build_schedule(tile_valid, tile_expert, tile_first, nkc, n_steps):
    n_tiles = tile_valid.shape[0]
    spt = jnp.where(jnp.logical_and(tile_valid > 0, tile_first > 0), nkc, 1).astype(I32)
    ends = jnp.cumsum(spt).astype(I32)
    starts = ends - spt
    total = ends[-1]
    s = jnp.arange(n_steps, dtype=I32)
    s_eff = jnp.minimum(s, total - 1)
    t = jnp.minimum(jnp.searchsorted(ends, s_eff, side="right").astype(I32), n_tiles - 1)
    valid = tile_valid[t] > 0
    first = jnp.logical_and(valid, tile_first[t] > 0)
    chunk = jnp.where(first, s_eff - starts[t], nkc - 1).astype(I32)
    kind = jnp.where(s < total, jnp.where(valid, jnp.where(first, 1, 2), 3), 0).astype(I32)
    return kind, t, tile_expert[t].astype(I32), chunk


def _dense_schedule(n_tiles, nkc, expert=0):
    valid = np.ones((n_tiles,), np.int32)
    first = np.zeros((n_tiles,), np.int32)
    first[0] = 1
    return _build_schedule(jnp.asarray(valid), jnp.full((n_tiles,), expert, I32), jnp.asarray(first),
                           nkc, n_tiles + nkc - 1)


def _silu_mul(g, u):
    return jax.nn.silu(g) * u


def _mm_body(kind_ref, tile_ref, exp_ref, chunk_ref, x_ref, *refs, n_w, nkc, kc, swiglu, has_bias,
             x_silu):
    w_refs = refs[:n_w]
    pos = n_w
    bias_ref = None
    if has_bias:
        bias_ref = refs[pos]
        pos += 1
    out_ref = refs[pos]
    wb_refs = refs[pos + 1:pos + 1 + n_w]
    acc_refs = refs[pos + 1 + n_w:pos + 1 + 2 * n_w]
    s = pl.program_id(1)
    kind = kind_ref[s]
    chunk = chunk_ref[s]

    def load_x(lo, hi):
        xs = x_ref[:, lo:hi]
        if x_silu:
            xs = jax.nn.silu(xs).astype(BF16)
        return xs

    def epilogue(vals):
        r = _silu_mul(vals[0], vals[1]) if swiglu else vals[0]
        if has_bias:
            r = r + bias_ref[...]
        return r.astype(out_ref.dtype)

    for ci in range(nkc):
        @pl.when(jnp.logical_and(kind == 1, chunk == ci))
        def _(ci=ci):
            lo = ci * kc
            xs = load_x(lo, lo + kc)
            for w_ref, wb_ref, acc_ref in zip(w_refs, wb_refs, acc_refs):
                wc = w_ref[...].astype(BF16)
                wb_ref[lo:lo + kc, :] = wc
                part = jnp.dot(xs, wc, preferred_element_type=F32)
                if ci == 0:
                    acc_ref[...] = part
                else:
                    acc_ref[...] += part
            if ci == nkc - 1:
                out_ref[...] = epilogue([a[...] for a in acc_refs])

    @pl.when(kind == 2)
    def _():
        xs = load_x(0, nkc * kc)
        out_ref[...] = epilogue([jnp.dot(xs, wb[...], preferred_element_type=F32) for wb in wb_refs])

    @pl.when(kind == 3)
    def _():
        out_ref[...] = jnp.zeros(out_ref.shape, out_ref.dtype)


def _matmul(x, ws, sched, *, tm, tn, kc, out_dtype, swiglu=False, bias=None, x_silu=False, name):
    n_rows, k_dim = x.shape
    n_cols = ws[0].shape[2]
    n_w = len(ws)
    nkc = k_dim // kc
    assert nkc * kc == k_dim and n_cols % tn == 0 and n_rows % tm == 0
    n_steps = sched[0].shape[0]
    imap_x = lambda j, s, kind, tile, exp, chunk: (tile[s], 0)
    imap_w = lambda j, s, kind, tile, exp, chunk: (exp[s], chunk[s], j)
    imap_o = lambda j, s, kind, tile, exp, chunk: (tile[s], j)
    in_specs = [pl.BlockSpec((tm, k_dim), imap_x)]
    in_specs += [pl.BlockSpec((None, kc, tn), imap_w) for _ in ws]
    args = [x] + list(ws)
    if bias is not None:
        in_specs.append(pl.BlockSpec((1, tn), lambda j, s, kind, tile, exp, chunk: (0, j)))
        args.append(bias.reshape(1, n_cols))
    body = functools.partial(_mm_body, n_w=n_w, nkc=nkc, kc=kc, swiglu=swiglu,
                             has_bias=bias is not None, x_silu=x_silu)
    return pl.pallas_call(
        body,
        grid_spec=pltpu.PrefetchScalarGridSpec(
            num_scalar_prefetch=4,
            grid=(n_cols // tn, n_steps),
            in_specs=in_specs,
            out_specs=pl.BlockSpec((tm, tn), imap_o),
            scratch_shapes=[pltpu.VMEM((k_dim, tn), BF16) for _ in ws]
            + [pltpu.VMEM((tm, tn), F32) for _ in ws]),
        out_shape=jax.ShapeDtypeStruct((n_rows, n_cols), out_dtype),
        compiler_params=_cparams(2),
        name=name,
    )(*sched, *args)


NORM_TILE = 128


def _row_mod_onehot(tile_rows):
    row = pl.program_id(0) * tile_rows + lax.broadcasted_iota(I32, (tile_rows, 1), 0)
    prompt_id = jnp.right_shift(row, int(math.log2(SEQ)))
    sample_id = BATCH + jnp.right_shift(row - M_PROMPT, int(math.log2(DEC_SEQ)))
    bid = jnp.minimum(jnp.where(row < M_PROMPT, prompt_id, sample_id), N_MOD_ROWS - 1)
    return (bid == lax.broadcasted_iota(I32, (1, N_MOD_ROWS), 1)).astype(F32)


def _rms(v):
    return v * lax.rsqrt(jnp.mean(v * v, axis=-1, keepdims=True) + EPS)


def _select_rows(onehot, mod_ref):
    return jnp.dot(onehot, mod_ref[...], preferred_element_type=F32, precision=lax.Precision.HIGHEST)


def _norm_body(*refs, has_f, has_next, router):
    refs = list(refs)
    take = lambda n: [refs.pop(0) for _ in range(n)]
    (x_ref,) = take(1)
    if has_f:
        f_ref, gpost_ref, gate_row_ref, gate_all_ref = take(4)
    if has_next:
        gnext_ref, shift_row_ref, shift_all_ref, scale_row_ref, scale_all_ref = take(5)
    if router:
        rwt_ref, rb_ref = take(2)
    if has_f:
        (xnew_ref,) = take(1)
    if has_next:
        (h_ref,) = take(1)
    if router:
        h32_ref, idx_ref, gate_ref = take(3)

    def run(select):
        x = x_ref[...]
        if has_f:
            fn = _rms(f_ref[...]) * gpost_ref[...]
            x = x + select(gate_row_ref, gate_all_ref) * fn
            xnew_ref[...] = x
        if has_next:
            y = _rms(x) * gnext_ref[...]
            h = y * (1.0 + select(scale_row_ref, scale_all_ref)) + select(shift_row_ref, shift_all_ref)
            h_ref[...] = h.astype(BF16)
        if router:
            h32_ref[...] = h
            logits = lax.dot_general(rwt_ref[...], h, (((1,), (1,)), ((), ())),
                                     preferred_element_type=F32,
                                     precision=lax.Precision.HIGHEST) + rb_ref[...]
            eidx = lax.broadcasted_iota(I32, logits.shape, 0)
            m1 = jnp.max(logits, axis=0, keepdims=True)
            i1 = jnp.min(jnp.where(logits == m1, eidx, N_EXPERTS), axis=0, keepdims=True)
            rest = jnp.where(eidx == i1, -jnp.inf, logits)
            m2 = jnp.max(rest, axis=0, keepdims=True)
            i2 = jnp.min(jnp.where(rest == m2, eidx, N_EXPERTS), axis=0, keepdims=True)
            e2 = jnp.exp(m2 - m1)
            denom = 1.0 + e2
            idx_ref[...] = jnp.concatenate([i1, i2], axis=0)
            gate_ref[...] = jnp.concatenate([1.0 / denom, e2 / denom], axis=0)

    @pl.when(pl.program_id(0) < M_PROMPT // NORM_TILE)
    def _():
        run(lambda row_ref, all_ref: row_ref[...])

    @pl.when(pl.program_id(0) >= M_PROMPT // NORM_TILE)
    def _():
        onehot = _row_mod_onehot(NORM_TILE)
        run(lambda row_ref, all_ref: _select_rows(onehot, all_ref))


def _norm(x, mod, *, f=None, g_post=None, gate_col=None, g_next=None, shift_col=None, scale_col=None,
          router_w=None, router_b=None, mod_next=None, name):
    if mod_next is None:
        mod_next = mod
    has_f, has_next, router = f is not None, g_next is not None, router_w is not None
    tiles_per_seq = SEQ // NORM_TILE
    row_spec = pl.BlockSpec((NORM_TILE, D_MODEL), lambda i: (i, 0))
    vec_spec = pl.BlockSpec((1, D_MODEL), lambda i: (0, 0))

    def mod_views(m, col):
        row_view = (m.reshape(N_MOD_ROWS, 1, 6 * D_MODEL),
                    pl.BlockSpec((None, 1, D_MODEL),
                                 lambda i, col=col: (jnp.minimum(i // tiles_per_seq, BATCH - 1), 0, col)))
        all_view = (m, pl.BlockSpec((N_MOD_ROWS, D_MODEL), lambda i, col=col: (0, col)))
        return [row_view, all_view]

    operands = [(x, row_spec)]
    out_shape, out_specs = [], []
    if has_f:
        operands += [(f, row_spec), (g_post.reshape(1, D_MODEL), vec_spec)] + mod_views(mod, gate_col)
        out_shape.append(jax.ShapeDtypeStruct((M_PAD, D_MODEL), F32))
        out_specs.append(row_spec)
    if has_next:
        operands += ([(g_next.reshape(1, D_MODEL), vec_spec)] + mod_views(mod_next, shift_col)
                     + mod_views(mod_next, scale_col))
        out_shape.append(jax.ShapeDtypeStruct((M_PAD, D_MODEL), BF16))
        out_specs.append(row_spec)
    if router:
        operands += [(router_w.T, pl.BlockSpec((N_EXPERTS, D_MODEL), lambda i: (0, 0))),
                     (router_b.reshape(N_EXPERTS, 1), pl.BlockSpec((N_EXPERTS, 1), lambda i: (0, 0)))]
        out_shape += [jax.ShapeDtypeStruct((M_PAD, D_MODEL), F32),
                      jax.ShapeDtypeStruct((TOP_K, M_PAD), I32),
                      jax.ShapeDtypeStruct((TOP_K, M_PAD), F32)]
        out_specs += [row_spec, pl.BlockSpec((TOP_K, NORM_TILE), lambda i: (0, i)),
                      pl.BlockSpec((TOP_K, NORM_TILE), lambda i: (0, i))]
    return pl.pallas_call(
        functools.partial(_norm_body, has_f=has_f, has_next=has_next, router=router),
        grid=(M_PAD // NORM_TILE,),
        in_specs=[spec for _, spec in operands], out_specs=out_specs, out_shape=out_shape,
        compiler_params=_cparams(1), name=name,
    )(*[arr for arr, _ in operands])


ROPE_TILE = 256


def _rope_body(q_ref, k_ref, v_ref, cos_ref, sin_ref, qb_ref, kf_ref, kb_ref, vb_ref):
    cos = cos_ref[...]
    sin = sin_ref[...]

    def rot(xg):
        return xg * cos + pltpu.roll(xg, HEAD_DIM // 2, axis=1) * sin

    for g in range(D_QK // HEAD_DIM):
        cols = slice(g * HEAD_DIM, (g + 1) * HEAD_DIM)
        qb_ref[:, cols] = rot(q_ref[:, cols]).astype(BF16)
        kr = rot(k_ref[:, cols])
        kf_ref[:, cols] = kr
        kb_ref[:, cols] = kr.astype(BF16)
    vb_ref[...] = v_ref[...].astype(BF16)


def _rope(proj, cos_tab, sin_tab):
    blk = lambda col: pl.BlockSpec((ROPE_TILE, D_QK), lambda i, col=col: (i, col))
    tab = pl.BlockSpec((ROPE_TILE, HEAD_DIM), lambda i: (i, 0))
    q_col = 2 * D_RNN // D_QK
    return pl.pallas_call(
        _rope_body,
        grid=(M_PAD // ROPE_TILE,),
        in_specs=[blk(q_col), blk(q_col + 1), blk(q_col + 2), tab, tab],
        out_specs=[blk(0), blk(0), blk(0), blk(0)],
        out_shape=[jax.ShapeDtypeStruct((M_PAD, D_QK), BF16), jax.ShapeDtypeStruct((M_PAD, D_QK), F32),
                   jax.ShapeDtypeStruct((M_PAD, D_QK), BF16), jax.ShapeDtypeStruct((M_PAD, D_QK), BF16)],
        compiler_params=_cparams(1), name="rope",
    )(proj, proj, proj, cos_tab, sin_tab)


LRU_COLS = 512


def _rglru_body(xr_ref, gr_ref, hist_ref, h0_ref, cw_ref, cb_ref, wa_ref, ba_ref, wx_ref, bx_ref,
                lam_ref, *refs, tc, aliased):
    if aliased:
        refs = refs[1:]
    y_ref, hl_ref, prev_scr, h_scr = refs
    t = pl.program_id(2)

    @pl.when(t == 0)
    def _():
        prev_scr[...] = hist_ref[0]
        h_scr[...] = h0_ref[0]

    x = xr_ref[...]
    xp = jnp.concatenate([prev_scr[...], x], axis=0)
    w = cw_ref[...]
    xc = cb_ref[...] + w[0:1] * pltpu.roll(xp, 3, axis=0)[SUBLANES:]
    xc = xc + w[1:2] * pltpu.roll(xp, 2, axis=0)[SUBLANES:]
    xc = xc + w[2:3] * pltpu.roll(xp, 1, axis=0)[SUBLANES:]
    xc = xc + w[3:4] * x

    ra, gi = [], []
    for n in range(LRU_COLS // LRU_BLOCK):
        xb = xc[:, n * LRU_BLOCK:(n + 1) * LRU_BLOCK].astype(BF16)
        ra.append(jnp.dot(xb, wa_ref[n].astype(BF16), preferred_element_type=F32))
        gi.append(jnp.dot(xb, wx_ref[n].astype(BF16), preferred_element_type=F32))
    r = jax.nn.sigmoid(jnp.concatenate(ra, axis=1) + ba_ref[...])
    gin = jax.nn.sigmoid(jnp.concatenate(gi, axis=1) + bx_ref[...])
    z = -lam_ref[...]
    softplus = jnp.maximum(z, 0.0) + jnp.log1p(jnp.exp(-jnp.abs(z)))
    log_a = -LRU_C * r * softplus
    a_cum = jnp.exp(log_a)
    one_minus_a2 = -jnp.tanh(log_a) * (jnp.exp(2.0 * log_a) + 1.0)
    b_cum = jnp.sqrt(one_minus_a2) * (gin * xc)

    rows = lax.broadcasted_iota(I32, a_cum.shape, 0)
    d = 1
    while d < tc:
        keep = rows >= d
        a_sh = jnp.where(keep, pltpu.roll(a_cum, d, axis=0), 1.0)
        b_sh = jnp.where(keep, pltpu.roll(b_cum, d, axis=0), 0.0)
        b_cum = a_cum * b_sh + b_cum
        a_cum = a_sh * a_cum
        d *= 2
    h = a_cum * h_scr[...] + b_cum
    y_ref[...] = (h * jax.nn.gelu(gr_ref[...])).astype(y_ref.dtype)
    h_scr[...] = h[tc - 1:tc]
    prev_scr[...] = x[tc - SUBLANES:tc]

    @pl.when(t == pl.num_programs(2) - 1)
    def _():
        hl_ref[0] = h[tc - 1:tc]


def _rglru(proj, *, row0, n_batch, seq, tc, hist8, h0, conv_w, conv_b, wa, ba, wx, bx, lam, buf, name):
    n_tc = seq // tc
    rb0 = row0 // tc
    n_cb = D_RNN // LRU_COLS
    nb = LRU_COLS // LRU_BLOCK
    row_blk = lambda off: pl.BlockSpec((tc, LRU_COLS), lambda b, c, t, off=off: (rb0 + b * n_tc + t, off + c))
    vec = lambda: pl.BlockSpec((1, LRU_COLS), lambda b, c, t: (0, c))
    in_specs = [
        row_blk(0), row_blk(n_cb),
        pl.BlockSpec((1, SUBLANES, LRU_COLS), lambda b, c, t: (b, 0, c)),
        pl.BlockSpec((1, 1, LRU_COLS), lambda b, c, t: (b, 0, c)),
        pl.BlockSpec((CONV_W, LRU_COLS), lambda b, c, t: (0, c)), vec(),
        pl.BlockSpec((nb, LRU_BLOCK, LRU_BLOCK), lambda b, c, t: (c, 0, 0)), vec(),
        pl.BlockSpec((nb, LRU_BLOCK, LRU_BLOCK), lambda b, c, t: (c, 0, 0)), vec(), vec()]
    args = [proj, proj, hist8, h0.reshape(n_batch, 1, D_RNN), conv_w, conv_b.reshape(1, D_RNN),
            wa, ba.reshape(1, D_RNN), wx, bx.reshape(1, D_RNN), lam.reshape(1, D_RNN)]
    aliases = {}
    if buf is not None:
        in_specs.append(pl.BlockSpec(memory_space=pl.ANY))
        args.append(buf)
        aliases = {len(args) - 1: 0}
        y_shape = jax.ShapeDtypeStruct(buf.shape, buf.dtype)
        y_spec = pl.BlockSpec((tc, LRU_COLS), lambda b, c, t: (rb0 + b * n_tc + t, c))
    else:
        y_shape = jax.ShapeDtypeStruct((n_batch * seq, D_RNN), F32)
        y_spec = pl.BlockSpec((tc, LRU_COLS), lambda b, c, t: (b * n_tc + t, c))
    return pl.pallas_call(
        functools.partial(_rglru_body, tc=tc, aliased=buf is not None),
        grid=(n_batch, n_cb, n_tc),
        in_specs=in_specs,
        out_specs=[y_spec, pl.BlockSpec((1, 1, LRU_COLS), lambda b, c, t: (b, 0, c))],
        out_shape=[y_shape, jax.ShapeDtypeStruct((n_batch, 1, D_RNN), F32)],
        scratch_shapes=[pltpu.VMEM((SUBLANES, LRU_COLS), F32), pltpu.VMEM((1, LRU_COLS), F32)],
        input_output_aliases=aliases,
        compiler_params=_cparams(3), name=name,
    )(*args)


def _diff_lambda(lamv_ref, lam_init):
    lv = lamv_ref[...]
    s1 = jnp.sum(lv[0:1] * lv[1:2], axis=-1, keepdims=True)
    s2 = jnp.sum(lv[2:3] * lv[3:4], axis=-1, keepdims=True)
    return jnp.exp(s1) - jnp.exp(s2) + lam_init


def _diff_finalize(o0, o1, lam, sg, lam_init):
    o = o0 - lam * o1
    return (_rms(o) * sg) * (1.0 - lam_init)


def _nt_dot(a, b):
    return lax.dot_general(a, b, (((1,), (1,)), ((), ())), preferred_element_type=F32)


ATT_TILE = 512


def _flash_body(q_ref, k_ref, v_ref, lamv_ref, sg_ref, buf_ref, o_ref, m_scr, l_scr, acc_scr, *, lam_init):
    i = pl.program_id(2)
    j = pl.program_id(3)

    @pl.when(j == 0)
    def _():
        m_scr[...] = jnp.full(m_scr.shape, NEG_INF, F32)
        l_scr[...] = jnp.zeros(l_scr.shape, F32)
        acc_scr[...] = jnp.zeros(acc_scr.shape, F32)

    def update(masked):
        v = v_ref[...]
        for c in range(2):
            cols = slice(c * HEAD_DIM, (c + 1) * HEAD_DIM)
            s = _nt_dot(q_ref[:, cols], k_ref[:, cols]) * ATT_SCALE
            if masked:
                row = lax.broadcasted_iota(I32, s.shape, 0)
                col = lax.broadcasted_iota(I32, s.shape, 1)
                s = jnp.where(col <= row, s, NEG_INF)
            m_old = m_scr[c]
            m_new = jnp.maximum(m_old, jnp.max(s, axis=-1, keepdims=True))
            alpha = jnp.exp(m_old - m_new)
            p = jnp.exp(s - m_new)
            l_scr[c] = alpha * l_scr[c] + jnp.sum(p, axis=-1, keepdims=True)
            acc_scr[c] = alpha * acc_scr[c] + jnp.dot(p.astype(BF16), v, preferred_element_type=F32)
            m_scr[c] = m_new

    @pl.when(j < i)
    def _():
        update(False)

    @pl.when(j == i)
    def _():
        update(True)
        lam = _diff_lambda(lamv_ref, lam_init)
        o = _diff_finalize(acc_scr[0] / l_scr[0], acc_scr[1] / l_scr[1], lam, sg_ref[...], lam_init)
        o_ref[...] = o.astype(o_ref.dtype)


def _flash(q_bf, k_bf, v_bf, lamv, subln_g, buf, lam_init):
    n_qt = SEQ // ATT_TILE
    q_spec = pl.BlockSpec((ATT_TILE, V_DIM), lambda b, h, i, j: (b * n_qt + i, h))
    kv_spec = pl.BlockSpec((ATT_TILE, V_DIM), lambda b, h, i, j: (b * n_qt + jnp.minimum(i, j), h))
    return pl.pallas_call(
        functools.partial(_flash_body, lam_init=lam_init),
        grid=(BATCH, N_HEADS, n_qt, n_qt),
        in_specs=[q_spec, kv_spec, kv_spec,
                  pl.BlockSpec((4, HEAD_DIM), lambda b, h, i, j: (0, 0)),
                  pl.BlockSpec((1, V_DIM), lambda b, h, i, j: (0, 0)),
                  pl.BlockSpec(memory_space=pl.ANY)],
        out_specs=pl.BlockSpec((ATT_TILE, V_DIM), lambda b, h, i, j: (b * n_qt + i, D_RNN // V_DIM + h)),
        out_shape=jax.ShapeDtypeStruct(buf.shape, buf.dtype),
        scratch_shapes=[pltpu.VMEM((2, ATT_TILE, 1), F32), pltpu.VMEM((2, ATT_TILE, 1), F32),
                        pltpu.VMEM((2, ATT_TILE, V_DIM), F32)],
        input_output_aliases={5: 0},
        compiler_params=_cparams(4), name="flash_prompt",
    )(q_bf, k_bf, v_bf, lamv, subln_g.reshape(1, V_DIM), buf)


N_HM = 2 * N_HEADS
Q_COLS = DEC_SEQ * N_HM


def _paged_body(pt_ref, qt_ref, k_ref, v_ref, kn_ref, vn_ref, lamv_ref, sg_ref, o_ref,
                m_scr, l_scr, acc_scr, bias_scr, p_scr, *, lam_init, n_pages):
    p = pl.program_id(1)
    k_rows = PAGE_SIZE * N_HM

    @pl.when(p == 0)
    def _():
        m_scr[...] = jnp.full(m_scr.shape, NEG_INF, F32)
        l_scr[...] = jnp.zeros(l_scr.shape, F32)
        acc_scr[...] = jnp.zeros(acc_scr.shape, F32)
        row = lax.broadcasted_iota(I32, (k_rows, Q_COLS), 0)
        col = lax.broadcasted_iota(I32, (k_rows, Q_COLS), 1)
        same_hm = jnp.bitwise_and(row, N_HM - 1) == jnp.bitwise_and(col, N_HM - 1)
        bias_scr[...] = jnp.where(same_hm, 0.0, NEG_INF)

    qt = qt_ref[0]

    def attend(k_rows_f32, v_rows_f32, bias):
        n = k_rows_f32.shape[0]
        st = jnp.dot(k_rows_f32.astype(BF16), qt, preferred_element_type=F32) * ATT_SCALE + bias
        m_old = m_scr[...]
        m_new = jnp.maximum(m_old, jnp.max(st, axis=0, keepdims=True))
        alpha = jnp.exp(m_old - m_new)
        pr = jnp.exp(st - m_new)
        l_scr[...] = alpha * l_scr[...] + jnp.sum(pr, axis=0, keepdims=True)
        m_scr[...] = m_new
        p_scr[0:n, :] = pr
        pair = p_scr[pl.ds(0, n // 2, stride=2), :] + p_scr[pl.ds(1, n // 2, stride=2), :]
        ot = lax.dot_general(v_rows_f32.astype(BF16), pair.astype(BF16), (((0,), (0,)), ((), ())),
                             preferred_element_type=F32)
        acc_scr[...] = alpha * acc_scr[...] + ot

    attend(k_ref[...], v_ref[...], bias_scr[...])

    @pl.when(p == n_pages - 1)
    def _():
        n_new = DEC_SEQ * N_HM
        row = lax.broadcasted_iota(I32, (n_new, Q_COLS), 0)
        col = lax.broadcasted_iota(I32, (n_new, Q_COLS), 1)
        shift = int(math.log2(N_HM))
        keep = jnp.logical_and(jnp.bitwise_and(row, N_HM - 1) == jnp.bitwise_and(col, N_HM - 1),
                               jnp.right_shift(row, shift) <= jnp.right_shift(col, shift))
        attend(kn_ref[0], vn_ref[0], jnp.where(keep, 0.0, NEG_INF))
        lam = _diff_lambda(lamv_ref, lam_init)
        on = acc_scr[...] / l_scr[...]
        d = on - lam * pltpu.roll(on, Q_COLS - 1, axis=1)
        inv = lax.rsqrt(jnp.mean(d * d, axis=0, keepdims=True) + EPS)
        o_ref[0] = ((d * inv) * sg_ref[...]) * (1.0 - lam_init)


def _paged(qt_s, cache_k, cache_v, layer, page_table, k_new, v_new, lamv, subln_g, lam_init):
    n_pages = page_table.shape[1]
    k_rows, v_rows = cache_k.shape[2], cache_v.shape[2]
    kpage_spec = pl.BlockSpec((None, None, k_rows, HEAD_DIM),
                              lambda b, p, pt: (layer, pt[b * n_pages + p], 0, 0))
    vpage_spec = pl.BlockSpec((None, None, v_rows, V_DIM),
                              lambda b, p, pt: (layer, pt[b * n_pages + p], 0, 0))
    per_batch = lambda shape: pl.BlockSpec((1,) + shape, lambda b, p, pt: (b, 0, 0))
    return pl.pallas_call(
        functools.partial(_paged_body, lam_init=lam_init, n_pages=n_pages),
        grid_spec=pltpu.PrefetchScalarGridSpec(
            num_scalar_prefetch=1,
            grid=(DEC_BATCH, n_pages),
            in_specs=[per_batch((HEAD_DIM, Q_COLS)), kpage_spec, vpage_spec,
                      per_batch((DEC_SEQ * N_HM, HEAD_DIM)), per_batch((DEC_SEQ * N_HEADS, V_DIM)),
                      pl.BlockSpec((4, HEAD_DIM), lambda b, p, pt: (0, 0)),
                      pl.BlockSpec((V_DIM, 1), lambda b, p, pt: (0, 0))],
            out_specs=per_batch((V_DIM, Q_COLS)),
            scratch_shapes=[pltpu.VMEM((1, Q_COLS), F32), pltpu.VMEM((1, Q_COLS), F32),
                            pltpu.VMEM((V_DIM, Q_COLS), F32),
                            pltpu.VMEM((k_rows, Q_COLS), F32), pltpu.VMEM((k_rows, Q_COLS), F32)]),
        out_shape=jax.ShapeDtypeStruct((DEC_BATCH, V_DIM, Q_COLS), F32),
        compiler_params=_cparams(2), name="paged_sample",
    )(page_table.reshape(-1), qt_s, cache_k, cache_v, k_new, v_new, lamv, subln_g.reshape(V_DIM, 1))


def _place_body(rows_ref, buf_ref, o_ref):
    o_ref[...] = rows_ref[...]


def _place_rows(buf, rows, row0):
    n = rows.shape[0]
    assert row0 % n == 0
    return pl.pallas_call(
        _place_body,
        grid=(1,),
        in_specs=[pl.BlockSpec(rows.shape, lambda i: (0, 0)), pl.BlockSpec(memory_space=pl.ANY)],
        out_specs=pl.BlockSpec(rows.shape, lambda i: (row0 // n, 0)),
        out_shape=jax.ShapeDtypeStruct(buf.shape, buf.dtype),
        input_output_aliases={1: 0},
        compiler_params=_cparams(1), name="place_rows",
    )(rows, buf)


N_ASSIGN = TOP_K * M_REAL
MOE_TILES = (N_ASSIGN + N_EXPERTS * (TOKEN_TILE - 1)) // TOKEN_TILE
MOE_ROWS = MOE_TILES * TOKEN_TILE
GATHER_TILE = 256
COMBINE_TILE = 128


def _row_copy(src_hbm, row, dst_vmem, slot, sem):
    return pltpu.make_async_copy(src_hbm.at[pl.ds(row, 1), :], dst_vmem.at[pl.ds(slot, 1), :], sem)


def _gather_body(src_ref, x_hbm, o_ref, buf, sem):
    def start(r, carry):
        _row_copy(x_hbm, src_ref[0, r], buf, r, sem).start()
        return carry

    def wait(r, carry):
        _row_copy(x_hbm, src_ref[0, r], buf, r, sem).wait()
        return carry

    lax.fori_loop(0, GATHER_TILE, start, 0)
    lax.fori_loop(0, GATHER_TILE, wait, 0)
    o_ref[...] = buf[...].astype(BF16)


def _gather_rows(src_rows, x32):
    n_t = MOE_ROWS // GATHER_TILE
    return pl.pallas_call(
        _gather_body,
        grid=(n_t,),
        in_specs=[pl.BlockSpec((None, 1, GATHER_TILE), lambda i: (i, 0, 0), memory_space=pltpu.SMEM),
                  pl.BlockSpec(memory_space=pl.ANY)],
        out_specs=pl.BlockSpec((GATHER_TILE, D_MODEL), lambda i: (i, 0)),
        out_shape=jax.ShapeDtypeStruct((MOE_ROWS, D_MODEL), BF16),
        scratch_shapes=[pltpu.VMEM((GATHER_TILE, D_MODEL), F32), pltpu.SemaphoreType.DMA(())],
        compiler_params=_cparams(1), name="moe_gather",
    )(src_rows.reshape(n_t, 1, GATHER_TILE), x32)


def _combine_body(pos_ref, ys_hbm, g_ref, o_ref, buf, sem):
    def start(r, carry):
        for k in range(TOP_K):
            _row_copy(ys_hbm, pos_ref[k, r], buf.at[k], r, sem).start()
        return carry

    def wait(r, carry):
        for k in range(TOP_K):
            _row_copy(ys_hbm, pos_ref[k, r], buf.at[k], r, sem).wait()
        return carry

    lax.fori_loop(0, COMBINE_TILE, start, 0)
    lax.fori_loop(0, COMBINE_TILE, wait, 0)
    g = g_ref[...]
    o_ref[...] = g[:, 0:1] * buf[0] + g[:, 1:2] * buf[1]


def _combine_rows(pos, gates_t, ys):
    n_t = M_PAD // COMBINE_TILE
    pos_blk = pos.reshape(TOP_K, n_t, COMBINE_TILE).transpose(1, 0, 2)
    return pl.pallas_call(
        _combine_body,
        grid=(n_t,),
        in_specs=[pl.BlockSpec((None, TOP_K, COMBINE_TILE), lambda i: (i, 0, 0), memory_space=pltpu.SMEM),
                  pl.BlockSpec(memory_space=pl.ANY),
                  pl.BlockSpec((COMBINE_TILE, TOP_K), lambda i: (i, 0))],
        out_specs=pl.BlockSpec((COMBINE_TILE, D_MODEL), lambda i: (i, 0)),
        out_shape=jax.ShapeDtypeStruct((M_PAD, D_MODEL), F32),
        scratch_shapes=[pltpu.VMEM((TOP_K, COMBINE_TILE, D_MODEL), F32), pltpu.SemaphoreType.DMA(())],
        compiler_params=_cparams(1), name="moe_combine",
    )(pos_blk, ys, gates_t)


def _moe_plan(idx):
    e_a = idx[:, :M_REAL].reshape(-1)
    counts = jnp.sum((e_a[:, None] == jnp.arange(N_EXPERTS, dtype=I32)[None, :]).astype(I32), axis=0)
    tiles_e = (counts + TOKEN_TILE - 1) // TOKEN_TILE
    tile_end = jnp.cumsum(tiles_e).astype(I32)
    tile_start = tile_end - tiles_e
    order = jnp.argsort(e_a, stable=True).astype(I32)
    sorted_pos = jnp.zeros((N_ASSIGN,), I32).at[order].set(jnp.arange(N_ASSIGN, dtype=I32))
    first_sorted = (jnp.cumsum(counts) - counts).astype(I32)
    dest = tile_start[e_a] * TOKEN_TILE + sorted_pos - first_sorted[e_a]
    token = jnp.tile(jnp.arange(M_REAL, dtype=I32), TOP_K)
    src_rows = jnp.zeros((MOE_ROWS,), I32).at[dest].set(token)
    pos = jnp.zeros((TOP_K, M_PAD), I32).at[:, :M_REAL].set(dest.reshape(TOP_K, M_REAL))
    tl = jnp.arange(MOE_TILES, dtype=I32)
    tl_used = jnp.minimum(tl, tile_end[-1] - 1)
    tile_expert = jnp.minimum(jnp.searchsorted(tile_end, tl_used, side="right").astype(I32), N_EXPERTS - 1)
    tile_valid = (tl < tile_end[-1]).astype(I32)
    tile_first = jnp.logical_and(tile_valid > 0, tl == tile_start[tile_expert]).astype(I32)
    return src_rows, pos, tile_valid, tile_expert, tile_first


PROJ_TN, PROJ_KC = 1024, 1024
GU_TN, GU_KC = 512, 1024
DOWN_TM, DOWN_TN, DOWN_KC = 256, 512, 2048


def _ffn(h_rows, wg, wu, wd, sched_gu, sched_down, tag):
    hmid = _matmul(h_rows, [wg, wu], sched_gu, tm=TOKEN_TILE, tn=GU_TN, kc=GU_KC, out_dtype=BF16,
                   swiglu=True, name=tag + "_gate_up")
    return _matmul(hmid, [wd], sched_down, tm=DOWN_TM, tn=DOWN_TN, kc=DOWN_KC, out_dtype=F32,
                   name=tag + "_down")


def _split_tiles(valid, expert, first, parts):
    sub_first = jnp.stack([first] + [jnp.zeros_like(first)] * (parts - 1), axis=1).reshape(-1)
    return jnp.repeat(valid, parts), jnp.repeat(expert, parts), sub_first


def kernel(x_prompt, x_sample, c_prompt, c_sample, cache_k, cache_v, page_table, state_h, state_conv, w_in, conv_w, conv_b, lru_wa, lru_ba, lru_wx, lru_bx, lru_lambda, lambda_q1, lambda_k1, lambda_q2, lambda_k2, subln_g, w_out, w_mod, b_mod, g_pre_mix, g_post_mix, g_pre_ffn, g_post_ffn, w_gate, w_up, w_down, router_w, router_b, we_gate, we_up, we_down):
    n_pages = page_table.shape[1]
    past_len = n_pages * cache_k.shape[2]

    x = jnp.concatenate([x_prompt.reshape(M_PROMPT, D_MODEL), x_sample.reshape(M_SAMPLE, D_MODEL),
                         jnp.zeros((M_PAD - M_REAL, D_MODEL), F32)], axis=0)
    c_rows = jnp.concatenate([c_prompt, c_sample,
                              jnp.zeros((N_MOD_ROWS - BATCH - DEC_BATCH, D_MODEL), F32)], axis=0)

    pos_rows = np.concatenate([np.tile(np.arange(SEQ), BATCH),
                               np.tile(past_len + np.arange(DEC_SEQ), DEC_BATCH),
                               np.zeros((M_PAD - M_REAL,))]).astype(np.float32)
    half = HEAD_DIM // 2
    inv_freq = ROPE_THETA ** (-jnp.arange(half, dtype=F32) / half)
    ang = jnp.asarray(pos_rows)[:, None] * inv_freq[None, :]
    cos_tab = jnp.concatenate([jnp.cos(ang), jnp.cos(ang)], axis=1)
    sin_tab = jnp.concatenate([-jnp.sin(ang), jnp.sin(ang)], axis=1)

    n_tok_tiles = M_PAD // TOKEN_TILE
    sched_k4096 = [_dense_schedule(n_tok_tiles, D_MODEL // PROJ_KC, l) for l in range(DEPTH)]
    sched_gu = [_dense_schedule(n_tok_tiles, D_MODEL // GU_KC, i) for i in range(w_gate.shape[0])]
    sched_down = [_dense_schedule(M_PAD // DOWN_TM, D_FF // DOWN_KC, i) for i in range(w_down.shape[0])]
    sched_mod = [_dense_schedule(1, D_MODEL // PROJ_KC, l) for l in range(DEPTH)]

    mods = [_matmul(c_rows, [w_mod], sched_mod[l], tm=N_MOD_ROWS, tn=PROJ_TN, kc=PROJ_KC,
                    out_dtype=F32, bias=b_mod[l], x_silu=True, name=f"mod{l}") for l in range(DEPTH)]
    moe_w = [w.reshape((-1,) + w.shape[2:]) for w in (we_gate, we_up, we_down)]

    hm = _norm(x, mods[0], g_next=g_pre_mix[0], shift_col=0, scale_col=1, name="pre_mix0")[0]

    zeros_hist = jnp.zeros((BATCH, SUBLANES, D_RNN), F32)
    zeros_h0 = jnp.zeros((BATCH, D_RNN), F32)
    cache_k2 = cache_k.reshape(cache_k.shape[0], cache_k.shape[1], -1, HEAD_DIM)
    cache_v2 = cache_v.reshape(cache_v.shape[0], cache_v.shape[1], -1, V_DIM)

    ks, vs, hs, cs = [], [], [], []
    for l in range(DEPTH):
        mod = mods[l]
        lam_init = 0.8 - 0.6 * math.exp(-0.3 * l)
        proj = _matmul(hm, [w_in], sched_k4096[l], tm=TOKEN_TILE, tn=PROJ_TN, kc=PROJ_KC,
                       out_dtype=F32, name=f"w_in{l}")
        q_bf, k_f32, k_bf, v_bf = _rope(proj, cos_tab, sin_tab)

        lru = dict(conv_w=conv_w[l], conv_b=conv_b[l], wa=lru_wa[l], ba=lru_ba[l], wx=lru_wx[l],
                   bx=lru_bx[l], lam=lru_lambda[l])
        buf = jnp.zeros((M_PAD, D_MODEL), BF16)
        buf, h_last_p = _rglru(proj, row0=0, n_batch=BATCH, seq=SEQ, tc=256, hist8=zeros_hist,
                               h0=zeros_h0, buf=buf, name=f"rglru_prompt{l}", **lru)
        hist_s = jnp.pad(state_conv[l], ((0, 0), (SUBLANES - (CONV_W - 1), 0), (0, 0)))
        y_s, h_last_s = _rglru(proj, row0=M_PROMPT, n_batch=DEC_BATCH, seq=DEC_SEQ, tc=DEC_SEQ,
                               hist8=hist_s, h0=state_h[l], buf=None, name=f"rglru_sample{l}", **lru)

        lamv = jnp.stack([lambda_q1[l], lambda_k1[l], lambda_q2[l], lambda_k2[l]], axis=0)
        buf = _flash(q_bf, k_bf, v_bf, lamv, subln_g[l], buf, lam_init)
        q_s = q_bf[M_PROMPT:M_REAL].reshape(DEC_BATCH, Q_COLS, HEAD_DIM)
        k_s = k_f32[M_PROMPT:M_REAL].reshape(DEC_BATCH, DEC_SEQ * N_HM, HEAD_DIM)
        v_s = proj[M_PROMPT:M_REAL, 2 * D_RNN + 2 * D_QK:].reshape(DEC_BATCH, DEC_SEQ * N_HEADS, V_DIM)
        o_t = _paged(jnp.swapaxes(q_s, 1, 2), cache_k2, cache_v2, l, page_table, k_s, v_s,
                     lamv, subln_g[l], lam_init)
        o_s = jnp.transpose(o_t.reshape(DEC_BATCH, V_DIM, DEC_SEQ, N_HEADS, 2)[..., 0], (0, 2, 3, 1))
        mix_s = jnp.concatenate([y_s, o_s.reshape(M_SAMPLE, D_ATT)], axis=1).astype(BF16)
        buf = _place_rows(buf, mix_s, M_PROMPT)

        m = _matmul(buf, [w_out], sched_k4096[l], tm=TOKEN_TILE, tn=PROJ_TN, kc=PROJ_KC,
                    out_dtype=F32, name=f"w_out{l}")
        i = l // 2
        if l % 2 == 0:
            x, hf = _norm(x, mod, f=m, g_post=g_post_mix[l], gate_col=2, g_next=g_pre_ffn[l],
                          shift_col=3, scale_col=4, name=f"post_mix{l}")
            f = _ffn(hf, w_gate, w_up, w_down, sched_gu[i], sched_down[i], f"ffn{l}")
        else:
            x, hf, hf32, idx, gates = _norm(x, mod, f=m, g_post=g_post_mix[l], gate_col=2,
                                            g_next=g_pre_ffn[l], shift_col=3, scale_col=4,
                                            router_w=router_w[i], router_b=router_b[i],
                                            name=f"post_mix{l}")
            src_rows, pos, t_valid, t_expert, t_first = _moe_plan(idx)
            t_expert = t_expert + i * N_EXPERTS
            xs = _gather_rows(src_rows, hf32)
            nkc_gu, nkc_dn = D_MODEL // GU_KC, D_FF // DOWN_KC
            moe_gu = _build_schedule(t_valid, t_expert, t_first, nkc_gu,
                                     MOE_TILES + N_EXPERTS * (nkc_gu - 1))
            parts = TOKEN_TILE // DOWN_TM
            moe_dn = _build_schedule(*_split_tiles(t_valid, t_expert, t_first, parts), nkc_dn,
                                     MOE_TILES * parts + N_EXPERTS * (nkc_dn - 1))
            ys = _ffn(xs, moe_w[0], moe_w[1], moe_w[2], moe_gu, moe_dn, f"moe{l}")
            gates_t = jnp.where(jnp.arange(M_PAD)[:, None] < M_REAL, gates.T, 0.0)
            f = _combine_rows(pos, gates_t, ys)
        if l + 1 < DEPTH:
            x, hm = _norm(x, mod, f=f, g_post=g_post_ffn[l], gate_col=5, g_next=g_pre_mix[l + 1],
                          shift_col=0, scale_col=1, mod_next=mods[l + 1], name=f"post_ffn{l}")
        else:
            x = _norm(x, mod, f=f, g_post=g_post_ffn[l], gate_col=5, name=f"post_ffn{l}")[0]

        ks.append(k_f32)
        vs.append(proj[:, 2 * D_RNN + 2 * D_QK:])
        hs.append((h_last_p, h_last_s))
        cs.append(proj[:, :D_RNN])

    def prompt_rows(a):
        return a[:M_PROMPT]

    def sample_rows(a):
        return a[M_PROMPT:M_REAL]

    y_prompt = prompt_rows(x).reshape(BATCH, SEQ, D_MODEL)
    y_sample = sample_rows(x).reshape(DEC_BATCH, DEC_SEQ, D_MODEL)
    k_prompt = jnp.stack([prompt_rows(k).reshape(BATCH, SEQ, N_HEADS, 2, HEAD_DIM) for k in ks])
    k_sample = jnp.stack([sample_rows(k).reshape(DEC_BATCH, DEC_SEQ, N_HEADS, 2, HEAD_DIM) for k in ks])
    v_prompt = jnp.stack([prompt_rows(v).reshape(BATCH, SEQ, N_HEADS, V_DIM) for v in vs])
    v_sample = jnp.stack([sample_rows(v).reshape(DEC_BATCH, DEC_SEQ, N_HEADS, V_DIM) for v in vs])
    h_prompt = jnp.stack([h[0].reshape(BATCH, D_RNN) for h in hs])
    h_sample = jnp.stack([h[1].reshape(DEC_BATCH, D_RNN) for h in hs])
    conv_prompt = jnp.stack([prompt_rows(c).reshape(BATCH, SEQ, D_RNN)[:, SEQ - (CONV_W - 1):] for c in cs])
    conv_sample = jnp.stack([sample_rows(c).reshape(DEC_BATCH, DEC_SEQ, D_RNN)[:, DEC_SEQ - (CONV_W - 1):]
                             for c in cs])
    return (y_prompt, y_sample, k_prompt, v_prompt, h_prompt, conv_prompt,
            k_sample, v_sample, h_sample, conv_sample)
```

```python
import functools
import math

import numpy as np
import jax
import jax.numpy as jnp
from jax import lax
from jax.experimental import pallas as pl
from jax.experimental.pallas import tpu as pltpu

F32 = jnp.float32
BF16 = jnp.bfloat16
I32 = jnp.int32

D_MODEL = 4096
BATCH = 4
SEQ = 2048
DEPTH = 2
DEC_BATCH = 8
DEC_SEQ = 8
PAGE_SIZE = 128
D_RNN = D_MODEL // 2
N_LRU_BLOCKS = 16
LRU_BLOCK = D_RNN // N_LRU_BLOCKS
CONV_W = 4
LRU_C = 8.0
D_ATT = D_MODEL - D_RNN
HEAD_DIM = 128
N_HEADS = D_ATT // (2 * HEAD_DIM)
V_DIM = 2 * HEAD_DIM
D_QK = N_HEADS * 2 * HEAD_DIM
D_IN = 2 * D_RNN + 2 * D_QK + N_HEADS * V_DIM
ROPE_THETA = 10000.0
D_FF = (D_MODEL * 7) // 2
N_EXPERTS = 8
TOP_K = 2
EPS = 1e-6
NEG_INF = -1e30
ATT_SCALE = HEAD_DIM ** -0.5

M_PROMPT = BATCH * SEQ
M_SAMPLE = DEC_BATCH * DEC_SEQ
M_REAL = M_PROMPT + M_SAMPLE
TOKEN_TILE = 512
M_PAD = -(-M_REAL // TOKEN_TILE) * TOKEN_TILE
N_MOD_ROWS = 16

LANES = 128
SUBLANES = 8
VMEM_LIMIT_BYTES = 56 * 1024 * 1024


def _cparams(n_grid_dims):
    return pltpu.CompilerParams(
        dimension_semantics=("arbitrary",) * n_grid_dims,
        vmem_limit_bytes=VMEM_LIMIT_BYTES)


def _build_schedule(tile_valid, tile_expert, tile_first, nkc, n_steps):
    n_tiles = tile_valid.shape[0]
    spt = jnp.where(jnp.logical_and(tile_valid > 0, tile_first > 0), nkc, 1).astype(I32)
    ends = jnp.cumsum(spt).astype(I32)
    starts = ends - spt
    total = ends[-1]
    s = jnp.arange(n_steps, dtype=I32)
    s_eff = jnp.minimum(s, total - 1)
    t = jnp.minimum(jnp.searchsorted(ends, s_eff, side="right").astype(I32), n_tiles - 1)
    valid = tile_valid[t] > 0
    first = jnp.logical_and(valid, tile_first[t] > 0)
    chunk = jnp.where(first, s_eff - starts[t], nkc - 1).astype(I32)
    kind = jnp.where(s < total, jnp.where(valid, jnp.where(first, 1, 2), 3), 0).astype(I32)
    return kind, t, tile_expert[t].astype(I32), chunk


def _dense_schedule(n_tiles, nkc, expert=0):
    valid = np.ones((n_tiles,), np.int32)
    first = np.zeros((n_tiles,), np.int32)
    first[0] = 1
    return _build_schedule(jnp.asarray(valid), jnp.full((n_tiles,), expert, I32), jnp.asarray(first),
                           nkc, n_tiles + nkc - 1)


def _silu_mul(g, u):
    return jax.nn.silu(g) * u


def _mm_body(kind_ref, tile_ref, exp_ref, chunk_ref, x_ref, *refs, n_w, nkc, kc, swiglu, has_bias,
             x_silu):
    w_refs = refs[:n_w]
    pos = n_w
    bias_ref = None
    if has_bias:
        bias_ref = refs[pos]
        pos += 1
    out_ref = refs[pos]
    wb_refs = refs[pos + 1:pos + 1 + n_w]
    acc_refs = refs[pos + 1 + n_w:pos + 1 + 2 * n_w]
    s = pl.program_id(1)
    kind = kind_ref[s]
    chunk = chunk_ref[s]

    def load_x(lo, hi):
        xs = x_ref[:, lo:hi]
        if x_silu:
            xs = jax.nn.silu(xs).astype(BF16)
        return xs

    def epilogue(vals):
        r = _silu_mul(vals[0], vals[1]) if swiglu else vals[0]
        if has_bias:
            r = r + bias_ref[...]
        return r.astype(out_ref.dtype)

    for ci in range(nkc):
        @pl.when(jnp.logical_and(kind == 1, chunk == ci))
        def _(ci=ci):
            lo = ci * kc
            xs = load_x(lo, lo + kc)
            for w_ref, wb_ref, acc_ref in zip(w_refs, wb_refs, acc_refs):
                wc = w_ref[...].astype(BF16)
                wb_ref[lo:lo + kc, :] = wc
                part = jnp.dot(xs, wc, preferred_element_type=F32)
                if ci == 0:
                    acc_ref[...] = part
                else:
                    acc_ref[...] += part
            if ci == nkc - 1:
                out_ref[...] = epilogue([a[...] for a in acc_refs])

    @pl.when(kind == 2)
    def _():
        xs = load_x(0, nkc * kc)
        out_ref[...] = epilogue([jnp.dot(xs, wb[...], preferred_element_type=F32) for wb in wb_refs])

    @pl.when(kind == 3)
    def _():
        out_ref[...] = jnp.zeros(out_ref.shape, out_ref.dtype)


def _matmul(x, ws, sched, *, tm, tn, kc, out_dtype, swiglu=False, bias=None, x_silu=False, name):
    n_rows, k_dim = x.shape
    n_cols = ws[0].shape[2]
    n_w = len(ws)
    nkc = k_dim // kc
    assert nkc * kc == k_dim and n_cols % tn == 0 and n_rows % tm == 0
    n_steps = sched[0].shape[0]
    imap_x = lambda j, s, kind, tile, exp, chunk: (tile[s], 0)
    imap_w = lambda j, s, kind, tile, exp, chunk: (exp[s], chunk[s], j)
    imap_o = lambda j, s, kind, tile, exp, chunk: (tile[s], j)
    in_specs = [pl.BlockSpec((tm, k_dim), imap_x)]
    in_specs += [pl.BlockSpec((None, kc, tn), imap_w) for _ in ws]
    args = [x] + list(ws)
    if bias is not None:
        in_specs.append(pl.BlockSpec((1, tn), lambda j, s, kind, tile, exp, chunk: (0, j)))
        args.append(bias.reshape(1, n_cols))
    body = functools.partial(_mm_body, n_w=n_w, nkc=nkc, kc=kc, swiglu=swiglu,
                             has_bias=bias is not None, x_silu=x_silu)
    return pl.pallas_call(
        body,
        grid_spec=pltpu.PrefetchScalarGridSpec(
            num_scalar_prefetch=4,
            grid=(n_cols // tn, n_steps),
            in_specs=in_specs,
            out_specs=pl.BlockSpec((tm, tn), imap_o),
            scratch_shapes=[pltpu.VMEM((k_dim, tn), BF16) for _ in ws]
            + [pltpu.VMEM((tm, tn), F32) for _ in ws]),
        out_shape=jax.ShapeDtypeStruct((n_rows, n_cols), out_dtype),
        compiler_params=_cparams(2),
        name=name,
    )(*sched, *args)


NORM_TILE = 128


def _row_mod_onehot(tile_rows):
    row = pl.program_id(0) * tile_rows + lax.broadcasted_iota(I32, (tile_rows, 1), 0)
    prompt_id = jnp.right_shift(row, int(math.log2(SEQ)))
    sample_id = BATCH + jnp.right_shift(row - M_PROMPT, int(math.log2(DEC_SEQ)))
    bid = jnp.minimum(jnp.where(row < M_PROMPT, prompt_id, sample_id), N_MOD_ROWS - 1)
    return (bid == lax.broadcasted_iota(I32, (1, N_MOD_ROWS), 1)).astype(F32)


def _rms(v):
    return v * lax.rsqrt(jnp.mean(v * v, axis=-1, keepdims=True) + EPS)


def _select_rows(onehot, mod_ref):
    return jnp.dot(onehot, mod_ref[...], preferred_element_type=F32, precision=lax.Precision.HIGHEST)


def _norm_body(*refs, has_f, has_next, router):
    refs = list(refs)
    take = lambda n: [refs.pop(0) for _ in range(n)]
    (x_ref,) = take(1)
    if has_f:
        f_ref, gpost_ref, gate_row_ref, gate_all_ref = take(4)
    if has_next:
        gnext_ref, shift_row_ref, shift_all_ref, scale_row_ref, scale_all_ref = take(5)
    if router:
        rwt_ref, rb_ref = take(2)
    if has_f:
        (xnew_ref,) = take(1)
    if has_next:
        (h_ref,) = take(1)
    if router:
        h32_ref, idx_ref, gate_ref = take(3)

    def run(select):
        x = x_ref[...]
        if has_f:
            fn = _rms(f_ref[...]) * gpost_ref[...]
            x = x + select(gate_row_ref, gate_all_ref) * fn
            xnew_ref[...] = x
        if has_next:
            y = _rms(x) * gnext_ref[...]
            h = y * (1.0 + select(scale_row_ref, scale_all_ref)) + select(shift_row_ref, shift_all_ref)
            h_ref[...] = h.astype(BF16)
        if router:
            h32_ref[...] = h
            logits = lax.dot_general(rwt_ref[...], h, (((1,), (1,)), ((), ())),
                                     preferred_element_type=F32,
                                     precision=lax.Precision.HIGHEST) + rb_ref[...]
            eidx = lax.broadcasted_iota(I32, logits.shape, 0)
            m1 = jnp.max(logits, axis=0, keepdims=True)
            i1 = jnp.min(jnp.where(logits == m1, eidx, N_EXPERTS), axis=0, keepdims=True)
            rest = jnp.where(eidx == i1, -jnp.inf, logits)
            m2 = jnp.max(rest, axis=0, keepdims=True)
            i2 = jnp.min(jnp.where(rest == m2, eidx, N_EXPERTS), axis=0, keepdims=True)
            e2 = jnp.exp(m2 - m1)
            denom = 1.0 + e2
            idx_ref[...] = jnp.concatenate([i1, i2], axis=0)
            gate_ref[...] = jnp.concatenate([1.0 / denom, e2 / denom], axis=0)

    @pl.when(pl.program_id(0) < M_PROMPT // NORM_TILE)
    def _():
        run(lambda row_ref, all_ref: row_ref[...])

    @pl.when(pl.program_id(0) >= M_PROMPT // NORM_TILE)
    def _():
        onehot = _row_mod_onehot(NORM_TILE)
        run(lambda row_ref, all_ref: _select_rows(onehot, all_ref))


def _norm(x, mod, *, f=None, g_post=None, gate_col=None, g_next=None, shift_col=None, scale_col=None,
          router_w=None, router_b=None, mod_next=None, name):
    if mod_next is None:
        mod_next = mod
    has_f, has_next, router = f is not None, g_next is not None, router_w is not None
    tiles_per_seq = SEQ // NORM_TILE
    row_spec = pl.BlockSpec((NORM_TILE, D_MODEL), lambda i: (i, 0))
    vec_spec = pl.BlockSpec((1, D_MODEL), lambda i: (0, 0))

    def mod_views(m, col):
        row_view = (m.reshape(N_MOD_ROWS, 1, 6 * D_MODEL),
                    pl.BlockSpec((None, 1, D_MODEL),
                                 lambda i, col=col: (jnp.minimum(i // tiles_per_seq, BATCH - 1), 0, col)))
        all_view = (m, pl.BlockSpec((N_MOD_ROWS, D_MODEL), lambda i, col=col: (0, col)))
        return [row_view, all_view]

    operands = [(x, row_spec)]
    out_shape, out_specs = [], []
    if has_f:
        operands += [(f, row_spec), (g_post.reshape(1, D_MODEL), vec_spec)] + mod_views(mod, gate_col)
        out_shape.append(jax.ShapeDtypeStruct((M_PAD, D_MODEL), F32))
        out_specs.append(row_spec)
    if has_next:
        operands += ([(g_next.reshape(1, D_MODEL), vec_spec)] + mod_views(mod_next, shift_col)
                     + mod_views(mod_next, scale_col))
        out_shape.append(jax.ShapeDtypeStruct((M_PAD, D_MODEL), BF16))
        out_specs.append(row_spec)
    if router:
        operands += [(router_w.T, pl.BlockSpec((N_EXPERTS, D_MODEL), lambda i: (0, 0))),
                     (router_b.reshape(N_EXPERTS, 1), pl.BlockSpec((N_EXPERTS, 1), lambda i: (0, 0)))]
        out_shape += [jax.ShapeDtypeStruct((M_PAD, D_MODEL), F32),
                      jax.ShapeDtypeStruct((TOP_K, M_PAD), I32),
                      jax.ShapeDtypeStruct((TOP_K, M_PAD), F32)]
        out_specs += [row_spec, pl.BlockSpec((TOP_K, NORM_TILE), lambda i: (0, i)),
                      pl.BlockSpec((TOP_K, NORM_TILE), lambda i: (0, i))]
    return pl.pallas_call(
        functools.partial(_norm_body, has_f=has_f, has_next=has_next, router=router),
        grid=(M_PAD // NORM_TILE,),
        in_specs=[spec for _, spec in operands], out_specs=out_specs, out_shape=out_shape,
        compiler_params=_cparams(1), name=name,
    )(*[arr for arr, _ in operands])


ROPE_TILE = 256


def _rope_body(q_ref, k_ref, v_ref, cos_ref, sin_ref, qb_ref, kf_ref, kb_ref, vb_ref):
    cos = cos_ref[...]
    sin = sin_ref[...]

    def rot(xg):
        return xg * cos + pltpu.roll(xg, HEAD_DIM // 2, axis=1) * sin

    for g in range(D_QK // HEAD_DIM):
        cols = slice(g * HEAD_DIM, (g + 1) * HEAD_DIM)
        qb_ref[:, cols] = rot(q_ref[:, cols]).astype(BF16)
        kr = rot(k_ref[:, cols])
        kf_ref[:, cols] = kr
        kb_ref[:, cols] = kr.astype(BF16)
    vb_ref[...] = v_ref[...].astype(BF16)


def _rope(proj, cos_tab, sin_tab):
    blk = lambda col: pl.BlockSpec((ROPE_TILE, D_QK), lambda i, col=col: (i, col))
    tab = pl.BlockSpec((ROPE_TILE, HEAD_DIM), lambda i: (i, 0))
    q_col = 2 * D_RNN // D_QK
    return pl.pallas_call(
        _rope_body,
        grid=(M_PAD // ROPE_TILE,),
        in_specs=[blk(q_col), blk(q_col + 1), blk(q_col + 2), tab, tab],
        out_specs=[blk(0), blk(0), blk(0), blk(0)],
        out_shape=[jax.ShapeDtypeStruct((M_PAD, D_QK), BF16), jax.ShapeDtypeStruct((M_PAD, D_QK), F32),
                   jax.ShapeDtypeStruct((M_PAD, D_QK), BF16), jax.ShapeDtypeStruct((M_PAD, D_QK), BF16)],
        compiler_params=_cparams(1), name="rope",
    )(proj, proj, proj, cos_tab, sin_tab)


LRU_COLS = 512


def _rglru_body(xr_ref, gr_ref, hist_ref, h0_ref, cw_ref, cb_ref, wa_ref, ba_ref, wx_ref, bx_ref,
                lam_ref, *refs, tc, aliased):
    if aliased:
        refs = refs[1:]
    y_ref, hl_ref, prev_scr, h_scr = refs
    t = pl.program_id(2)

    @pl.when(t == 0)
    def _():
        prev_scr[...] = hist_ref[0]
        h_scr[...] = h0_ref[0]

    x = xr_ref[...]
    xp = jnp.concatenate([prev_scr[...], x], axis=0)
    w = cw_ref[...]
    xc = cb_ref[...] + w[0:1] * pltpu.roll(xp, 3, axis=0)[SUBLANES:]
    xc = xc + w[1:2] * pltpu.roll(xp, 2, axis=0)[SUBLANES:]
    xc = xc + w[2:3] * pltpu.roll(xp, 1, axis=0)[SUBLANES:]
    xc = xc + w[3:4] * x

    ra, gi = [], []
    for n in range(LRU_COLS // LRU_BLOCK):
        xb = xc[:, n * LRU_BLOCK:(n + 1) * LRU_BLOCK].astype(BF16)
        ra.append(jnp.dot(xb, wa_ref[n].astype(BF16), preferred_element_type=F32))
        gi.append(jnp.dot(xb, wx_ref[n].astype(BF16), preferred_element_type=F32))
    r = jax.nn.sigmoid(jnp.concatenate(ra, axis=1) + ba_ref[...])
    gin = jax.nn.sigmoid(jnp.concatenate(gi, axis=1) + bx_ref[...])
    z = -lam_ref[...]
    softplus = jnp.maximum(z, 0.0) + jnp.log1p(jnp.exp(-jnp.abs(z)))
    log_a = -LRU_C * r * softplus
    a_cum = jnp.exp(log_a)
    one_minus_a2 = -jnp.tanh(log_a) * (jnp.exp(2.0 * log_a) + 1.0)
    b_cum = jnp.sqrt(one_minus_a2) * (gin * xc)

    rows = lax.broadcasted_iota(I32, a_cum.shape, 0)
    d = 1
    while d < tc:
        keep = rows >= d
        a_sh = jnp.where(keep, pltpu.roll(a_cum, d, axis=0), 1.0)
        b_sh = jnp.where(keep, pltpu.roll(b_cum, d, axis=0), 0.0)
        b_cum = a_cum * b_sh + b_cum
        a_cum = a_sh * a_cum
        d *= 2
    h = a_cum * h_scr[...] + b_cum
    y_ref[...] = (h * jax.nn.gelu(gr_ref[...])).astype(y_ref.dtype)
    h_scr[...] = h[tc - 1:tc]
    prev_scr[...] = x[tc - SUBLANES:tc]

    @pl.when(t == pl.num_programs(2) - 1)
    def _():
        hl_ref[0] = h[tc - 1:tc]


def _rglru(proj, *, row0, n_batch, seq, tc, hist8, h0, conv_w, conv_b, wa, ba, wx, bx, lam, buf, name):
    n_tc = seq // tc
    rb0 = row0 // tc
    n_cb = D_RNN // LRU_COLS
    nb = LRU_COLS // LRU_BLOCK
    row_blk = lambda off: pl.BlockSpec((tc, LRU_COLS), lambda b, c, t, off=off: (rb0 + b * n_tc + t, off + c))
    vec = lambda: pl.BlockSpec((1, LRU_COLS), lambda b, c, t: (0, c))
    in_specs = [
        row_blk(0), row_blk(n_cb),
        pl.BlockSpec((1, SUBLANES, LRU_COLS), lambda b, c, t: (b, 0, c)),
        pl.BlockSpec((1, 1, LRU_COLS), lambda b, c, t: (b, 0, c)),
        pl.BlockSpec((CONV_W, LRU_COLS), lambda b, c, t: (0, c)), vec(),
        pl.BlockSpec((nb, LRU_BLOCK, LRU_BLOCK), lambda b, c, t: (c, 0, 0)), vec(),
        pl.BlockSpec((nb, LRU_BLOCK, LRU_BLOCK), lambda b, c, t: (c, 0, 0)), vec(), vec()]
    args = [proj, proj, hist8, h0.reshape(n_batch, 1, D_RNN), conv_w, conv_b.reshape(1, D_RNN),
            wa, ba.reshape(1, D_RNN), wx, bx.reshape(1, D_RNN), lam.reshape(1, D_RNN)]
    aliases = {}
    if buf is not None:
        in_specs.append(pl.BlockSpec(memory_space=pl.ANY))
        args.append(buf)
        aliases = {len(args) - 1: 0}
        y_shape = jax.ShapeDtypeStruct(buf.shape, buf.dtype)
        y_spec = pl.BlockSpec((tc, LRU_COLS), lambda b, c, t: (rb0 + b * n_tc + t, c))
    else:
        y_shape = jax.ShapeDtypeStruct((n_batch * seq, D_RNN), F32)
        y_spec = pl.BlockSpec((tc, LRU_COLS), lambda b, c, t: (b * n_tc + t, c))
    return pl.pallas_call(
        functools.partial(_rglru_body, tc=tc, aliased=buf is not None),
        grid=(n_batch, n_cb, n_tc),
        in_specs=in_specs,
        out_specs=[y_spec, pl.BlockSpec((1, 1, LRU_COLS), lambda b, c, t: (b, 0, c))],
        out_shape=[y_shape, jax.ShapeDtypeStruct((n_batch, 1, D_RNN), F32)],
        scratch_shapes=[pltpu.VMEM((SUBLANES, LRU_COLS), F32), pltpu.VMEM((1, LRU_COLS), F32)],
        input_output_aliases=aliases,
        compiler_params=_cparams(3), name=name,
    )(*args)


def _diff_lambda(lamv_ref, lam_init):
    lv = lamv_ref[...]
    s1 = jnp.sum(lv[0:1] * lv[1:2], axis=-1, keepdims=True)
    s2 = jnp.sum(lv[2:3] * lv[3:4], axis=-1, keepdims=True)
    return jnp.exp(s1) - jnp.exp(s2) + lam_init


def _diff_finalize(o0, o1, lam, sg, lam_init):
    o = o0 - lam * o1
    return (_rms(o) * sg) * (1.0 - lam_init)


def _nt_dot(a, b):
    return lax.dot_general(a, b, (((1,), (1,)), ((), ())), preferred_element_type=F32)


ATT_TILE = 512


def _flash_body(q_ref, k_ref, v_ref, lamv_ref, sg_ref, buf_ref, o_ref, m_scr, l_scr, acc_scr, *, lam_init):
    i = pl.program_id(2)
    j = pl.program_id(3)

    @pl.when(j == 0)
    def _():
        m_scr[...] = jnp.full(m_scr.shape, NEG_INF, F32)
        l_scr[...] = jnp.zeros(l_scr.shape, F32)
        acc_scr[...] = jnp.zeros(acc_scr.shape, F32)

    def update(masked):
        v = v_ref[...]
        for c in range(2):
            cols = slice(c * HEAD_DIM, (c + 1) * HEAD_DIM)
            s = _nt_dot(q_ref[:, cols], k_ref[:, cols]) * ATT_SCALE
            if masked:
                row = lax.broadcasted_iota(I32, s.shape, 0)
                col = lax.broadcasted_iota(I32, s.shape, 1)
                s = jnp.where(col <= row, s, NEG_INF)
            m_old = m_scr[c]
            m_new = jnp.maximum(m_old, jnp.max(s, axis=-1, keepdims=True))
            alpha = jnp.exp(m_old - m_new)
            p = jnp.exp(s - m_new)
            l_scr[c] = alpha * l_scr[c] + jnp.sum(p, axis=-1, keepdims=True)
            acc_scr[c] = alpha * acc_scr[c] + jnp.dot(p.astype(BF16), v, preferred_element_type=F32)
            m_scr[c] = m_new

    @pl.when(j < i)
    def _():
        update(False)

    @pl.when(j == i)
    def _():
        update(True)
        lam = _diff_lambda(lamv_ref, lam_init)
        o = _diff_finalize(acc_scr[0] / l_scr[0], acc_scr[1] / l_scr[1], lam, sg_ref[...], lam_init)
        o_ref[...] = o.astype(o_ref.dtype)


def _flash(q_bf, k_bf, v_bf, lamv, subln_g, buf, lam_init):
    n_qt = SEQ // ATT_TILE
    q_spec = pl.BlockSpec((ATT_TILE, V_DIM), lambda b, h, i, j: (b * n_qt + i, h))
    kv_spec = pl.BlockSpec((ATT_TILE, V_DIM), lambda b, h, i, j: (b * n_qt + jnp.minimum(i, j), h))
    return pl.pallas_call(
        functools.partial(_flash_body, lam_init=lam_init),
        grid=(BATCH, N_HEADS, n_qt, n_qt),
        in_specs=[q_spec, kv_spec, kv_spec,
                  pl.BlockSpec((4, HEAD_DIM), lambda b, h, i, j: (0, 0)),
                  pl.BlockSpec((1, V_DIM), lambda b, h, i, j: (0, 0)),
                  pl.BlockSpec(memory_space=pl.ANY)],
        out_specs=pl.BlockSpec((ATT_TILE, V_DIM), lambda b, h, i, j: (b * n_qt + i, D_RNN // V_DIM + h)),
        out_shape=jax.ShapeDtypeStruct(buf.shape, buf.dtype),
        scratch_shapes=[pltpu.VMEM((2, ATT_TILE, 1), F32), pltpu.VMEM((2, ATT_TILE, 1), F32),
                        pltpu.VMEM((2, ATT_TILE, V_DIM), F32)],
        input_output_aliases={5: 0},
        compiler_params=_cparams(4), name="flash_prompt",
    )(q_bf, k_bf, v_bf, lamv, subln_g.reshape(1, V_DIM), buf)


N_HM = 2 * N_HEADS
Q_COLS = DEC_SEQ * N_HM


PAGES_PER_STEP = 2


def _paged_body(pt_ref, qt_ref, *refs, lam_init, n_steps):
    k_refs = refs[:PAGES_PER_STEP]
    v_refs = refs[PAGES_PER_STEP:2 * PAGES_PER_STEP]
    (kn_ref, vn_ref, lamv_ref, sg_ref, o_ref,
     m_scr, l_scr, acc_scr, bias_scr, p_scr) = refs[2 * PAGES_PER_STEP:]
    p = pl.program_id(1)
    k_rows = PAGE_SIZE * N_HM

    @pl.when(p == 0)
    def _():
        m_scr[...] = jnp.full(m_scr.shape, NEG_INF, F32)
        l_scr[...] = jnp.zeros(l_scr.shape, F32)
        acc_scr[...] = jnp.zeros(acc_scr.shape, F32)
        row = lax.broadcasted_iota(I32, (k_rows, Q_COLS), 0)
        col = lax.broadcasted_iota(I32, (k_rows, Q_COLS), 1)
        same_hm = jnp.bitwise_and(row, N_HM - 1) == jnp.bitwise_and(col, N_HM - 1)
        bias_scr[...] = jnp.where(same_hm, 0.0, NEG_INF)

    qt = qt_ref[0]

    def attend(blocks):
        scores = [jnp.dot(k.astype(BF16), qt, preferred_element_type=F32) * ATT_SCALE + bias
                  for k, _, bias in blocks]
        m_old = m_scr[...]
        m_new = m_old
        for st in scores:
            m_new = jnp.maximum(m_new, jnp.max(st, axis=0, keepdims=True))
        alpha = jnp.exp(m_old - m_new)
        l_new = alpha * l_scr[...]
        acc = alpha * acc_scr[...]
        for i, (st, (_, v, _)) in enumerate(zip(scores, blocks)):
            n = st.shape[0]
            pr = jnp.exp(st - m_new)
            l_new = l_new + jnp.sum(pr, axis=0, keepdims=True)
            p_scr[i, 0:n, :] = pr
            pair = p_scr[i, pl.ds(0, n // 2, stride=2), :] + p_scr[i, pl.ds(1, n // 2, stride=2), :]
            acc = acc + lax.dot_general(v.astype(BF16), pair.astype(BF16), (((0,), (0,)), ((), ())),
                                        preferred_element_type=F32)
        l_scr[...] = l_new
        m_scr[...] = m_new
        acc_scr[...] = acc

    attend([(k_ref[...], v_ref[...], bias_scr[...]) for k_ref, v_ref in zip(k_refs, v_refs)])

    @pl.when(p == n_steps - 1)
    def _():
        n_new = DEC_SEQ * N_HM
        row = lax.broadcasted_iota(I32, (n_new, Q_COLS), 0)
        col = lax.broadcasted_iota(I32, (n_new, Q_COLS), 1)
        shift = int(math.log2(N_HM))
        keep = jnp.logical_and(jnp.bitwise_and(row, N_HM - 1) == jnp.bitwise_and(col, N_HM - 1),
                               jnp.right_shift(row, shift) <= jnp.right_shift(col, shift))
        attend([(kn_ref[0], vn_ref[0], jnp.where(keep, 0.0, NEG_INF))])
        lam = _diff_lambda(lamv_ref, lam_init)
        on = acc_scr[...] / l_scr[...]
        d = on - lam * pltpu.roll(on, Q_COLS - 1, axis=1)
        inv = lax.rsqrt(jnp.mean(d * d, axis=0, keepdims=True) + EPS)
        o_ref[0] = ((d * inv) * sg_ref[...]) * (1.0 - lam_init)


def _paged(qt_s, cache_k, cache_v, layer, page_table, k_new, v_new, lamv, subln_g, lam_init):
    n_pages = page_table.shape[1]
    assert n_pages % PAGES_PER_STEP == 0
    n_steps = n_pages // PAGES_PER_STEP
    k_rows, v_rows = cache_k.shape[2], cache_v.shape[2]

    def page_spec(rows, width, i):
        return pl.BlockSpec((None, None, rows, width),
                            lambda b, p, pt, i=i: (layer, pt[b * n_pages + p * PAGES_PER_STEP + i], 0, 0))

    per_batch = lambda shape: pl.BlockSpec((1,) + shape, lambda b, p, pt: (b, 0, 0))
    return pl.pallas_call(
        functools.partial(_paged_body, lam_init=lam_init, n_steps=n_steps),
        grid_spec=pltpu.PrefetchScalarGridSpec(
            num_scalar_prefetch=1,
            grid=(DEC_BATCH, n_steps),
            in_specs=[per_batch((HEAD_DIM, Q_COLS))]
            + [page_spec(k_rows, HEAD_DIM, i) for i in range(PAGES_PER_STEP)]
            + [page_spec(v_rows, V_DIM, i) for i in range(PAGES_PER_STEP)]
            + [per_batch((DEC_SEQ * N_HM, HEAD_DIM)), per_batch((DEC_SEQ * N_HEADS, V_DIM)),
               pl.BlockSpec((4, HEAD_DIM), lambda b, p, pt: (0, 0)),
               pl.BlockSpec((V_DIM, 1), lambda b, p, pt: (0, 0))],
            out_specs=per_batch((V_DIM, Q_COLS)),
            scratch_shapes=[pltpu.VMEM((1, Q_COLS), F32), pltpu.VMEM((1, Q_COLS), F32),
                            pltpu.VMEM((V_DIM, Q_COLS), F32),
                            pltpu.VMEM((k_rows, Q_COLS), F32),
                            pltpu.VMEM((PAGES_PER_STEP, k_rows, Q_COLS), F32)]),
        out_shape=jax.ShapeDtypeStruct((DEC_BATCH, V_DIM, Q_COLS), F32),
        compiler_params=_cparams(2), name="paged_sample",
    )(page_table.reshape(-1), qt_s, *([cache_k] * PAGES_PER_STEP), *([cache_v] * PAGES_PER_STEP),
      k_new, v_new, lamv, subln_g.reshape(V_DIM, 1))


def _place_body(rows_ref, buf_ref, o_ref):
    o_ref[...] = rows_ref[...]


def _place_rows(buf, rows, row0):
    n = rows.shape[0]
    assert row0 % n == 0
    return pl.pallas_call(
        _place_body,
        grid=(1,),
        in_specs=[pl.BlockSpec(rows.shape, lambda i: (0, 0)), pl.BlockSpec(memory_space=pl.ANY)],
        out_specs=pl.BlockSpec(rows.shape, lambda i: (row0 // n, 0)),
        out_shape=jax.ShapeDtypeStruct(buf.shape, buf.dtype),
        input_output_aliases={1: 0},
        compiler_params=_cparams(1), name="place_rows",
    )(rows, buf)


N_ASSIGN = TOP_K * M_REAL
MOE_TILES = (N_ASSIGN + N_EXPERTS * (TOKEN_TILE - 1)) // TOKEN_TILE
MOE_ROWS = MOE_TILES * TOKEN_TILE
GATHER_TILE = 256
COMBINE_TILE = 128


def _row_copy(src_hbm, row, dst_vmem, slot, sem):
    return pltpu.make_async_copy(src_hbm.at[pl.ds(row, 1), :], dst_vmem.at[pl.ds(slot, 1), :], sem)


DMA_ISSUE_UNROLL = 8


def _gather_body(src_ref, x_hbm, o_ref, buf, sem):
    def start(g, carry):
        for u in range(DMA_ISSUE_UNROLL):
            r = g * DMA_ISSUE_UNROLL + u
            _row_copy(x_hbm, src_ref[0, r], buf, r, sem).start(priority=u % 2)
        return carry

    lax.fori_loop(0, GATHER_TILE // DMA_ISSUE_UNROLL, start, 0)
    for r in range(GATHER_TILE):
        _row_copy(x_hbm, 0, buf, r, sem).wait()
    o_ref[...] = buf[...].astype(BF16)


def _gather_rows(src_rows, x32):
    n_t = MOE_ROWS // GATHER_TILE
    return pl.pallas_call(
        _gather_body,
        grid=(n_t,),
        in_specs=[pl.BlockSpec((None, 1, GATHER_TILE), lambda i: (i, 0, 0), memory_space=pltpu.SMEM),
                  pl.BlockSpec(memory_space=pl.ANY)],
        out_specs=pl.BlockSpec((GATHER_TILE, D_MODEL), lambda i: (i, 0)),
        out_shape=jax.ShapeDtypeStruct((MOE_ROWS, D_MODEL), BF16),
        scratch_shapes=[pltpu.VMEM((GATHER_TILE, D_MODEL), F32), pltpu.SemaphoreType.DMA(())],
        compiler_params=_cparams(1), name="moe_gather",
    )(src_rows.reshape(n_t, 1, GATHER_TILE), x32)


def _combine_body(pos_ref, ys_hbm, g_ref, o_ref, buf, sem):
    def start(g, carry):
        for u in range(DMA_ISSUE_UNROLL):
            r = g * DMA_ISSUE_UNROLL + u
            for k in range(TOP_K):
                _row_copy(ys_hbm, pos_ref[k, r], buf.at[k], r, sem).start(priority=k)
        return carry

    lax.fori_loop(0, COMBINE_TILE // DMA_ISSUE_UNROLL, start, 0)
    for r in range(COMBINE_TILE):
        for k in range(TOP_K):
            _row_copy(ys_hbm, 0, buf.at[k], r, sem).wait()
    g = g_ref[...]
    o_ref[...] = g[:, 0:1] * buf[0] + g[:, 1:2] * buf[1]


def _combine_rows(pos, gates_t, ys):
    n_t = M_PAD // COMBINE_TILE
    pos_blk = pos.reshape(TOP_K, n_t, COMBINE_TILE).transpose(1, 0, 2)
    return pl.pallas_call(
        _combine_body,
        grid=(n_t,),
        in_specs=[pl.BlockSpec((None, TOP_K, COMBINE_TILE), lambda i: (i, 0, 0), memory_space=pltpu.SMEM),
                  pl.BlockSpec(memory_space=pl.ANY),
                  pl.BlockSpec((COMBINE_TILE, TOP_K), lambda i: (i, 0))],
        out_specs=pl.BlockSpec((COMBINE_TILE, D_MODEL), lambda i: (i, 0)),
        out_shape=jax.ShapeDtypeStruct((M_PAD, D_MODEL), F32),
        scratch_shapes=[pltpu.VMEM((TOP_K, COMBINE_TILE, D_MODEL), F32), pltpu.SemaphoreType.DMA(())],
        compiler_params=_cparams(1), name="moe_combine",
    )(pos_blk, ys, gates_t)


def _moe_plan(idx):
    e_a = idx[:, :M_REAL].reshape(-1)
    onehot = (e_a[:, None] == jnp.arange(N_EXPERTS, dtype=I32)[None, :]).astype(I32)
    running = jnp.cumsum(onehot, axis=0).astype(I32)
    counts = running[-1]
    rank = jnp.sum((running - onehot) * onehot, axis=1)
    tiles_e = (counts + TOKEN_TILE - 1) // TOKEN_TILE
    tile_end = jnp.cumsum(tiles_e).astype(I32)
    tile_start = tile_end - tiles_e
    dest = tile_start[e_a] * TOKEN_TILE + rank
    pos = jnp.pad(dest.reshape(TOP_K, M_REAL), ((0, 0), (0, M_PAD - M_REAL)))
    tl = jnp.arange(MOE_TILES, dtype=I32)
    tl_used = jnp.minimum(tl, tile_end[-1] - 1)
    tile_expert = jnp.minimum(jnp.searchsorted(tile_end, tl_used, side="right").astype(I32), N_EXPERTS - 1)
    tile_valid = (tl < tile_end[-1]).astype(I32)
    tile_first = jnp.logical_and(tile_valid > 0, tl == tile_start[tile_expert]).astype(I32)
    order = jnp.argsort(e_a, stable=True).astype(I32)
    first_sorted = (jnp.cumsum(counts) - counts).astype(I32)
    row = jnp.arange(MOE_ROWS, dtype=I32)
    row_tile = row // TOKEN_TILE
    row_expert = tile_expert[row_tile]
    offset = row - tile_start[row_expert] * TOKEN_TILE
    used = jnp.logical_and(tile_valid[row_tile] > 0, offset < counts[row_expert])
    assignment = order[jnp.clip(first_sorted[row_expert] + offset, 0, N_ASSIGN - 1)]
    src_rows = jnp.where(used, assignment % M_REAL, 0).astype(I32)
    return src_rows, pos, tile_valid, tile_expert, tile_first


PROJ_TN, PROJ_KC = 1024, 1024
GU_TN, GU_KC = 512, 1024
DOWN_TM, DOWN_TN, DOWN_KC = 256, 512, 2048


def _ffn(h_rows, wg, wu, wd, sched_gu, sched_down, tag):
    hmid = _matmul(h_rows, [wg, wu], sched_gu, tm=TOKEN_TILE, tn=GU_TN, kc=GU_KC, out_dtype=BF16,
                   swiglu=True, name=tag + "_gate_up")
    return _matmul(hmid, [wd], sched_down, tm=DOWN_TM, tn=DOWN_TN, kc=DOWN_KC, out_dtype=F32,
                   name=tag + "_down")


def _split_tiles(valid, expert, first, parts):
    sub_first = jnp.stack([first] + [jnp.zeros_like(first)] * (parts - 1), axis=1).reshape(-1)
    return jnp.repeat(valid, parts), jnp.repeat(expert, parts), sub_first


def kernel(x_prompt, x_sample, c_prompt, c_sample, cache_k, cache_v, page_table, state_h, state_conv, w_in, conv_w, conv_b, lru_wa, lru_ba, lru_wx, lru_bx, lru_lambda, lambda_q1, lambda_k1, lambda_q2, lambda_k2, subln_g, w_out, w_mod, b_mod, g_pre_mix, g_post_mix, g_pre_ffn, g_post_ffn, w_gate, w_up, w_down, router_w, router_b, we_gate, we_up, we_down):
    n_pages = page_table.shape[1]
    past_len = n_pages * cache_k.shape[2]

    x = jnp.concatenate([x_prompt.reshape(M_PROMPT, D_MODEL), x_sample.reshape(M_SAMPLE, D_MODEL),
                         jnp.zeros((M_PAD - M_REAL, D_MODEL), F32)], axis=0)
    c_rows = jnp.concatenate([c_prompt, c_sample,
                              jnp.zeros((N_MOD_ROWS - BATCH - DEC_BATCH, D_MODEL), F32)], axis=0)

    pos_rows = np.concatenate([np.tile(np.arange(SEQ), BATCH),
                               np.tile(past_len + np.arange(DEC_SEQ), DEC_BATCH),
                               np.zeros((M_PAD - M_REAL,))]).astype(np.float32)
    half = HEAD_DIM // 2
    inv_freq = ROPE_THETA ** (-jnp.arange(half, dtype=F32) / half)
    ang = jnp.asarray(pos_rows)[:, None] * inv_freq[None, :]
    cos_tab = jnp.concatenate([jnp.cos(ang), jnp.cos(ang)], axis=1)
    sin_tab = jnp.concatenate([-jnp.sin(ang), jnp.sin(ang)], axis=1)

    n_tok_tiles = M_PAD // TOKEN_TILE
    sched_k4096 = [_dense_schedule(n_tok_tiles, D_MODEL // PROJ_KC, l) for l in range(DEPTH)]
    sched_gu = [_dense_schedule(n_tok_tiles, D_MODEL // GU_KC, i) for i in range(w_gate.shape[0])]
    sched_down = [_dense_schedule(M_PAD // DOWN_TM, D_FF // DOWN_KC, i) for i in range(w_down.shape[0])]
    sched_mod = [_dense_schedule(1, D_MODEL // PROJ_KC, l) for l in range(DEPTH)]

    mods = [_matmul(c_rows, [w_mod], sched_mod[l], tm=N_MOD_ROWS, tn=PROJ_TN, kc=PROJ_KC,
                    out_dtype=F32, bias=b_mod[l], x_silu=True, name=f"mod{l}") for l in range(DEPTH)]
    moe_w = [w.reshape((-1,) + w.shape[2:]) for w in (we_gate, we_up, we_down)]

    hm = _norm(x, mods[0], g_next=g_pre_mix[0], shift_col=0, scale_col=1, name="pre_mix0")[0]

    zeros_hist = jnp.zeros((BATCH, SUBLANES, D_RNN), F32)
    zeros_h0 = jnp.zeros((BATCH, D_RNN), F32)
    cache_k2 = cache_k.reshape(cache_k.shape[0], cache_k.shape[1], -1, HEAD_DIM)
    cache_v2 = cache_v.reshape(cache_v.shape[0], cache_v.shape[1], -1, V_DIM)

    ks, vs, hs, cs = [], [], [], []
    for l in range(DEPTH):
        mod = mods[l]
        lam_init = 0.8 - 0.6 * math.exp(-0.3 * l)
        proj = _matmul(hm, [w_in], sched_k4096[l], tm=TOKEN_TILE, tn=PROJ_TN, kc=PROJ_KC,
                       out_dtype=F32, name=f"w_in{l}")
        q_bf, k_f32, k_bf, v_bf = _rope(proj, cos_tab, sin_tab)

        lru = dict(conv_w=conv_w[l], conv_b=conv_b[l], wa=lru_wa[l], ba=lru_ba[l], wx=lru_wx[l],
                   bx=lru_bx[l], lam=lru_lambda[l])
        buf = jnp.zeros((M_PAD, D_MODEL), BF16)
        buf, h_last_p = _rglru(proj, row0=0, n_batch=BATCH, seq=SEQ, tc=256, hist8=zeros_hist,
                               h0=zeros_h0, buf=buf, name=f"rglru_prompt{l}", **lru)
        hist_s = jnp.pad(state_conv[l], ((0, 0), (SUBLANES - (CONV_W - 1), 0), (0, 0)))
        y_s, h_last_s = _rglru(proj, row0=M_PROMPT, n_batch=DEC_BATCH, seq=DEC_SEQ, tc=DEC_SEQ,
                               hist8=hist_s, h0=state_h[l], buf=None, name=f"rglru_sample{l}", **lru)

        lamv = jnp.stack([lambda_q1[l], lambda_k1[l], lambda_q2[l], lambda_k2[l]], axis=0)
        buf = _flash(q_bf, k_bf, v_bf, lamv, subln_g[l], buf, lam_init)
        q_s = q_bf[M_PROMPT:M_REAL].reshape(DEC_BATCH, Q_COLS, HEAD_DIM)
        k_s = k_f32[M_PROMPT:M_REAL].reshape(DEC_BATCH, DEC_SEQ * N_HM, HEAD_DIM)
        v_s = proj[M_PROMPT:M_REAL, 2 * D_RNN + 2 * D_QK:].reshape(DEC_BATCH, DEC_SEQ * N_HEADS, V_DIM)
        o_t = _paged(jnp.swapaxes(q_s, 1, 2), cache_k2, cache_v2, l, page_table, k_s, v_s,
                     lamv, subln_g[l], lam_init)
        o_s = jnp.transpose(o_t.reshape(DEC_BATCH, V_DIM, DEC_SEQ, N_HEADS, 2)[..., 0], (0, 2, 3, 1))
        mix_s = jnp.concatenate([y_s, o_s.reshape(M_SAMPLE, D_ATT)], axis=1).astype(BF16)
        buf = _place_rows(buf, mix_s, M_PROMPT)

        m = _matmul(buf, [w_out], sched_k4096[l], tm=TOKEN_TILE, tn=PROJ_TN, kc=PROJ_KC,
                    out_dtype=F32, name=f"w_out{l}")
        i = l // 2
        if l % 2 == 0:
            x, hf = _norm(x, mod, f=m, g_post=g_post_mix[l], gate_col=2, g_next=g_pre_ffn[l],
                          shift_col=3, scale_col=4, name=f"post_mix{l}")
            f = _ffn(hf, w_gate, w_up, w_down, sched_gu[i], sched_down[i], f"ffn{l}")
        else:
            x, hf, hf32, idx, gates = _norm(x, mod, f=m, g_post=g_post_mix[l], gate_col=2,
                                            g_next=g_pre_ffn[l], shift_col=3, scale_col=4,
                                            router_w=router_w[i], router_b=router_b[i],
                                            name=f"post_mix{l}")
            src_rows, pos, t_valid, t_expert, t_first = _moe_plan(idx)
            t_expert = t_expert + i * N_EXPERTS
            xs = _gather_rows(src_rows, hf32)
            nkc_gu, nkc_dn = D_MODEL // GU_KC, D_FF // DOWN_KC
            moe_gu = _build_schedule(t_valid, t_expert, t_first, nkc_gu,
                                     MOE_TILES + N_EXPERTS * (nkc_gu - 1))
            parts = TOKEN_TILE // DOWN_TM
            moe_dn = _build_schedule(*_split_tiles(t_valid, t_expert, t_first, parts), nkc_dn,
                                     MOE_TILES * parts + N_EXPERTS * (nkc_dn - 1))
            ys = _ffn(xs, moe_w[0], moe_w[1], moe_w[2], moe_gu, moe_dn, f"moe{l}")
            gates_t = jnp.where(jnp.arange(M_PAD)[:, None] < M_REAL, gates.T, 0.0)
            f = _combine_rows(pos, gates_t, ys)
        if l + 1 < DEPTH:
            x, hm = _norm(x, mod, f=f, g_post=g_post_ffn[l], gate_col=5, g_next=g_pre_mix[l + 1],
                          shift_col=0, scale_col=1, mod_next=mods[l + 1], name=f"post_ffn{l}")
        else:
            x = _norm(x, mod, f=f, g_post=g_post_ffn[l], gate_col=5, name=f"post_ffn{l}")[0]

        ks.append(k_f32)
        vs.append(proj[:, 2 * D_RNN + 2 * D_QK:])
        hs.append((h_last_p, h_last_s))
        cs.append(proj[:, :D_RNN])

    def prompt_rows(a):
        return a[:M_PROMPT]

    def sample_rows(a):
        return a[M_PROMPT:M_REAL]

    y_prompt = prompt_rows(x).reshape(BATCH, SEQ, D_MODEL)
    y_sample = sample_rows(x).reshape(DEC_BATCH, DEC_SEQ, D_MODEL)
    k_prompt = jnp.stack([prompt_rows(k).reshape(BATCH, SEQ, N_HEADS, 2, HEAD_DIM) for k in ks])
    k_sample = jnp.stack([sample_rows(k).reshape(DEC_BATCH, DEC_SEQ, N_HEADS, 2, HEAD_DIM) for k in ks])
    v_prompt = jnp.stack([prompt_rows(v).reshape(BATCH, SEQ, N_HEADS, V_DIM) for v in vs])
    v_sample = jnp.stack([sample_rows(v).reshape(DEC_BATCH, DEC_SEQ, N_HEADS, V_DIM) for v in vs])
    h_prompt = jnp.stack([h[0].reshape(BATCH, D_RNN) for h in hs])
    h_sample = jnp.stack([h[1].reshape(DEC_BATCH, D_RNN) for h in hs])
    conv_prompt = jnp.stack([prompt_rows(c).reshape(BATCH, SEQ, D_RNN)[:, SEQ - (CONV_W - 1):] for c in cs])
    conv_sample = jnp.stack([sample_rows(c).reshape(DEC_BATCH, DEC_SEQ, D_RNN)[:, DEC_SEQ - (CONV_W - 1):]
                             for c in cs])
    return (y_prompt, y_sample, k_prompt, v_prompt, h_prompt, conv_prompt,
            k_sample, v_sample, h_sample, conv_sample)
```

```python
import functools
import math

import numpy as np
import jax
import jax.numpy as jnp
from jax import lax
from jax.experimental import pallas as pl
from jax.experimental.pallas import tpu as pltpu

F32 = jnp.float32
BF16 = jnp.bfloat16
I32 = jnp.int32

D_MODEL = 4096
BATCH = 4
SEQ = 2048
DEPTH = 2
DEC_BATCH = 8
DEC_SEQ = 8
PAGE_SIZE = 128
D_RNN = D_MODEL // 2
N_LRU_BLOCKS = 16
LRU_BLOCK = D_RNN // N_LRU_BLOCKS
CONV_W = 4
LRU_C = 8.0
D_ATT = D_MODEL - D_RNN
HEAD_DIM = 128
N_HEADS = D_ATT // (2 * HEAD_DIM)
V_DIM = 2 * HEAD_DIM
D_QK = N_HEADS * 2 * HEAD_DIM
D_IN = 2 * D_RNN + 2 * D_QK + N_HEADS * V_DIM
ROPE_THETA = 10000.0
D_FF = (D_MODEL * 7) // 2
N_EXPERTS = 8
TOP_K = 2
EPS = 1e-6
NEG_INF = -1e30
ATT_SCALE = HEAD_DIM ** -0.5

M_PROMPT = BATCH * SEQ
M_SAMPLE = DEC_BATCH * DEC_SEQ
M_REAL = M_PROMPT + M_SAMPLE
TOKEN_TILE = 512
M_PAD = -(-M_REAL // TOKEN_TILE) * TOKEN_TILE
N_MOD_ROWS = 16

LANES = 128
SUBLANES = 8
VMEM_LIMIT_BYTES = 56 * 1024 * 1024


def _cparams(n_grid_dims):
    return pltpu.CompilerParams(
        dimension_semantics=("arbitrary",) * n_grid_dims,
        vmem_limit_bytes=VMEM_LIMIT_BYTES)


def _build_schedule(tile_valid, tile_expert, tile_first, nkc, n_steps):
    n_tiles = tile_valid.shape[0]
    spt = jnp.where(jnp.logical_and(tile_valid > 0, tile_first > 0), nkc, 1).astype(I32)
    ends = jnp.cumsum(spt).astype(I32)
    starts = ends - spt
    total = ends[-1]
    s = jnp.arange(n_steps, dtype=I32)
    s_eff = jnp.minimum(s, total - 1)
    t = jnp.minimum(jnp.searchsorted(ends, s_eff, side="right").astype(I32), n_tiles - 1)
    valid = tile_valid[t] > 0
    first = jnp.logical_and(valid, tile_first[t] > 0)
    chunk = jnp.where(first, s_eff - starts[t], nkc - 1).astype(I32)
    kind = jnp.where(s < total, jnp.where(valid, jnp.where(first, 1, 2), 3), 0).astype(I32)
    return kind, t, tile_expert[t].astype(I32), chunk


def _dense_schedule(n_tiles, nkc, expert=0):
    valid = np.ones((n_tiles,), np.int32)
    first = np.zeros((n_tiles,), np.int32)
    first[0] = 1
    return _build_schedule(jnp.asarray(valid), jnp.full((n_tiles,), expert, I32), jnp.asarray(first),
                           nkc, n_tiles + nkc - 1)


def _silu_mul(g, u):
    return jax.nn.silu(g) * u


def _mm_body(kind_ref, tile_ref, exp_ref, chunk_ref, x_ref, *refs, n_w, nkc, kc, swiglu, has_bias,
             x_silu):
    w_refs = refs[:n_w]
    pos = n_w
    bias_ref = None
    if has_bias:
        bias_ref = refs[pos]
        pos += 1
    out_ref = refs[pos]
    wb_refs = refs[pos + 1:pos + 1 + n_w]
    acc_refs = refs[pos + 1 + n_w:pos + 1 + 2 * n_w]
    s = pl.program_id(1)
    kind = kind_ref[s]
    chunk = chunk_ref[s]

    def load_x(lo, hi):
        xs = x_ref[:, lo:hi]
        if x_silu:
            xs = jax.nn.silu(xs).astype(BF16)
        return xs

    def epilogue(vals):
        r = _silu_mul(vals[0], vals[1]) if swiglu else vals[0]
        if has_bias:
            r = r + bias_ref[...]
        return r.astype(out_ref.dtype)

    for ci in range(nkc):
        @pl.when(jnp.logical_and(kind == 1, chunk == ci))
        def _(ci=ci):
            lo = ci * kc
            xs = load_x(lo, lo + kc)
            for w_ref, wb_ref, acc_ref in zip(w_refs, wb_refs, acc_refs):
                wc = w_ref[...].astype(BF16)
                wb_ref[lo:lo + kc, :] = wc
                part = jnp.dot(xs, wc, preferred_element_type=F32)
                if ci == 0:
                    acc_ref[...] = part
                else:
                    acc_ref[...] += part
            if ci == nkc - 1:
                out_ref[...] = epilogue([a[...] for a in acc_refs])

    @pl.when(kind == 2)
    def _():
        xs = load_x(0, nkc * kc)
        out_ref[...] = epilogue([jnp.dot(xs, wb[...], preferred_element_type=F32) for wb in wb_refs])

    @pl.when(kind == 3)
    def _():
        out_ref[...] = jnp.zeros(out_ref.shape, out_ref.dtype)


def _matmul(x, ws, sched, *, tm, tn, kc, out_dtype, swiglu=False, bias=None, x_silu=False, name):
    n_rows, k_dim = x.shape
    n_cols = ws[0].shape[2]
    n_w = len(ws)
    nkc = k_dim // kc
    assert nkc * kc == k_dim and n_cols % tn == 0 and n_rows % tm == 0
    n_steps = sched[0].shape[0]
    imap_x = lambda j, s, kind, tile, exp, chunk: (tile[s], 0)
    imap_w = lambda j, s, kind, tile, exp, chunk: (exp[s], chunk[s], j)
    imap_o = lambda j, s, kind, tile, exp, chunk: (tile[s], j)
    in_specs = [pl.BlockSpec((tm, k_dim), imap_x)]
    in_specs += [pl.BlockSpec((None, kc, tn), imap_w) for _ in ws]
    args = [x] + list(ws)
    if bias is not None:
        in_specs.append(pl.BlockSpec((1, tn), lambda j, s, kind, tile, exp, chunk: (0, j)))
        args.append(bias.reshape(1, n_cols))
    body = functools.partial(_mm_body, n_w=n_w, nkc=nkc, kc=kc, swiglu=swiglu,
                             has_bias=bias is not None, x_silu=x_silu)
    return pl.pallas_call(
        body,
        grid_spec=pltpu.PrefetchScalarGridSpec(
            num_scalar_prefetch=4,
            grid=(n_cols // tn, n_steps),
            in_specs=in_specs,
            out_specs=pl.BlockSpec((tm, tn), imap_o),
            scratch_shapes=[pltpu.VMEM((k_dim, tn), BF16) for _ in ws]
            + [pltpu.VMEM((tm, tn), F32) for _ in ws]),
        out_shape=jax.ShapeDtypeStruct((n_rows, n_cols), out_dtype),
        compiler_params=_cparams(2),
        name=name,
    )(*sched, *args)


NORM_TILE = 128


def _row_mod_onehot(tile_rows):
    row = pl.program_id(0) * tile_rows + lax.broadcasted_iota(I32, (tile_rows, 1), 0)
    prompt_id = jnp.right_shift(row, int(math.log2(SEQ)))
    sample_id = BATCH + jnp.right_shift(row - M_PROMPT, int(math.log2(DEC_SEQ)))
    bid = jnp.minimum(jnp.where(row < M_PROMPT, prompt_id, sample_id), N_MOD_ROWS - 1)
    return (bid == lax.broadcasted_iota(I32, (1, N_MOD_ROWS), 1)).astype(F32)


def _rms(v):
    return v * lax.rsqrt(jnp.mean(v * v, axis=-1, keepdims=True) + EPS)


def _select_rows(onehot, mod_ref):
    return jnp.dot(onehot, mod_ref[...], preferred_element_type=F32, precision=lax.Precision.HIGHEST)


def _norm_body(*refs, has_f, has_next, router):
    refs = list(refs)
    take = lambda n: [refs.pop(0) for _ in range(n)]
    (x_ref,) = take(1)
    if has_f:
        f_ref, gpost_ref, gate_row_ref, gate_all_ref = take(4)
    if has_next:
        gnext_ref, shift_row_ref, shift_all_ref, scale_row_ref, scale_all_ref = take(5)
    if router:
        rwt_ref, rb_ref = take(2)
    if has_f:
        (xnew_ref,) = take(1)
    if has_next:
        (h_ref,) = take(1)
    if router:
        h32_ref, idx_ref, gate_ref = take(3)

    def run(select):
        x = x_ref[...]
        if has_f:
            fn = _rms(f_ref[...]) * gpost_ref[...]
            x = x + select(gate_row_ref, gate_all_ref) * fn
            xnew_ref[...] = x
        if has_next:
            y = _rms(x) * gnext_ref[...]
            h = y * (1.0 + select(scale_row_ref, scale_all_ref)) + select(shift_row_ref, shift_all_ref)
            h_ref[...] = h.astype(BF16)
        if router:
            h32_ref[...] = h
            logits = lax.dot_general(rwt_ref[...], h, (((1,), (1,)), ((), ())),
                                     preferred_element_type=F32,
                                     precision=lax.Precision.HIGHEST) + rb_ref[...]
            eidx = lax.broadcasted_iota(I32, logits.shape, 0)
            m1 = jnp.max(logits, axis=0, keepdims=True)
            i1 = jnp.min(jnp.where(logits == m1, eidx, N_EXPERTS), axis=0, keepdims=True)
            rest = jnp.where(eidx == i1, -jnp.inf, logits)
            m2 = jnp.max(rest, axis=0, keepdims=True)
            i2 = jnp.min(jnp.where(rest == m2, eidx, N_EXPERTS), axis=0, keepdims=True)
            e2 = jnp.exp(m2 - m1)
            denom = 1.0 + e2
            idx_ref[...] = jnp.concatenate([i1, i2], axis=0)
            gate_ref[...] = jnp.concatenate([1.0 / denom, e2 / denom], axis=0)

    @pl.when(pl.program_id(0) < M_PROMPT // NORM_TILE)
    def _():
        run(lambda row_ref, all_ref: row_ref[...])

    @pl.when(pl.program_id(0) >= M_PROMPT // NORM_TILE)
    def _():
        onehot = _row_mod_onehot(NORM_TILE)
        run(lambda row_ref, all_ref: _select_rows(onehot, all_ref))


def _norm(x, mod, *, f=None, g_post=None, gate_col=None, g_next=None, shift_col=None, scale_col=None,
          router_w=None, router_b=None, mod_next=None, name):
    if mod_next is None:
        mod_next = mod
    has_f, has_next, router = f is not None, g_next is not None, router_w is not None
    tiles_per_seq = SEQ // NORM_TILE
    row_spec = pl.BlockSpec((NORM_TILE, D_MODEL), lambda i: (i, 0))
    vec_spec = pl.BlockSpec((1, D_MODEL), lambda i: (0, 0))

    def mod_views(m, col):
        row_view = (m.reshape(N_MOD_ROWS, 1, 6 * D_MODEL),
                    pl.BlockSpec((None, 1, D_MODEL),
                                 lambda i, col=col: (jnp.minimum(i // tiles_per_seq, BATCH - 1), 0, col)))
        all_view = (m, pl.BlockSpec((N_MOD_ROWS, D_MODEL), lambda i, col=col: (0, col)))
        return [row_view, all_view]

    operands = [(x, row_spec)]
    out_shape, out_specs = [], []
    if has_f:
        operands += [(f, row_spec), (g_post.reshape(1, D_MODEL), vec_spec)] + mod_views(mod, gate_col)
        out_shape.append(jax.ShapeDtypeStruct((M_PAD, D_MODEL), F32))
        out_specs.append(row_spec)
    if has_next:
        operands += ([(g_next.reshape(1, D_MODEL), vec_spec)] + mod_views(mod_next, shift_col)
                     + mod_views(mod_next, scale_col))
        out_shape.append(jax.ShapeDtypeStruct((M_PAD, D_MODEL), BF16))
        out_specs.append(row_spec)
    if router:
        operands += [(router_w.T, pl.BlockSpec((N_EXPERTS, D_MODEL), lambda i: (0, 0))),
                     (router_b.reshape(N_EXPERTS, 1), pl.BlockSpec((N_EXPERTS, 1), lambda i: (0, 0)))]
        out_shape += [jax.ShapeDtypeStruct((M_PAD, D_MODEL), F32),
                      jax.ShapeDtypeStruct((TOP_K, M_PAD), I32),
                      jax.ShapeDtypeStruct((TOP_K, M_PAD), F32)]
        out_specs += [row_spec, pl.BlockSpec((TOP_K, NORM_TILE), lambda i: (0, i)),
                      pl.BlockSpec((TOP_K, NORM_TILE), lambda i: (0, i))]
    return pl.pallas_call(
        functools.partial(_norm_body, has_f=has_f, has_next=has_next, router=router),
        grid=(M_PAD // NORM_TILE,),
        in_specs=[spec for _, spec in operands], out_specs=out_specs, out_shape=out_shape,
        compiler_params=_cparams(1), name=name,
    )(*[arr for arr, _ in operands])


ROPE_TILE = 256


def _rope_body(q_ref, k_ref, v_ref, cos_ref, sin_ref, qb_ref, kf_ref, kb_ref, vb_ref):
    cos = cos_ref[...]
    sin = sin_ref[...]

    def rot(xg):
        return xg * cos + pltpu.roll(xg, HEAD_DIM // 2, axis=1) * sin

    for g in range(D_QK // HEAD_DIM):
        cols = slice(g * HEAD_DIM, (g + 1) * HEAD_DIM)
        qb_ref[:, cols] = rot(q_ref[:, cols]).astype(BF16)
        kr = rot(k_ref[:, cols])
        kf_ref[:, cols] = kr
        kb_ref[:, cols] = kr.astype(BF16)
    vb_ref[...] = v_ref[...].astype(BF16)


def _rope(proj, cos_tab, sin_tab):
    blk = lambda col: pl.BlockSpec((ROPE_TILE, D_QK), lambda i, col=col: (i, col))
    tab = pl.BlockSpec((ROPE_TILE, HEAD_DIM), lambda i: (i, 0))
    q_col = 2 * D_RNN // D_QK
    return pl.pallas_call(
        _rope_body,
        grid=(M_PAD // ROPE_TILE,),
        in_specs=[blk(q_col), blk(q_col + 1), blk(q_col + 2), tab, tab],
        out_specs=[blk(0), blk(0), blk(0), blk(0)],
        out_shape=[jax.ShapeDtypeStruct((M_PAD, D_QK), BF16), jax.ShapeDtypeStruct((M_PAD, D_QK), F32),
                   jax.ShapeDtypeStruct((M_PAD, D_QK), BF16), jax.ShapeDtypeStruct((M_PAD, D_QK), BF16)],
        compiler_params=_cparams(1), name="rope",
    )(proj, proj, proj, cos_tab, sin_tab)


LRU_COLS = 512


def _rglru_body(xr_ref, gr_ref, hist_ref, h0_ref, cw_ref, cb_ref, wa_ref, ba_ref, wx_ref, bx_ref,
                lam_ref, *refs, tc, aliased):
    if aliased:
        refs = refs[1:]
    y_ref, hl_ref, prev_scr, h_scr = refs
    t = pl.program_id(2)

    @pl.when(t == 0)
    def _():
        prev_scr[...] = hist_ref[0]
        h_scr[...] = h0_ref[0]

    x = xr_ref[...]
    xp = jnp.concatenate([prev_scr[...], x], axis=0)
    w = cw_ref[...]
    xc = cb_ref[...] + w[0:1] * pltpu.roll(xp, 3, axis=0)[SUBLANES:]
    xc = xc + w[1:2] * pltpu.roll(xp, 2, axis=0)[SUBLANES:]
    xc = xc + w[2:3] * pltpu.roll(xp, 1, axis=0)[SUBLANES:]
    xc = xc + w[3:4] * x

    ra, gi = [], []
    for n in range(LRU_COLS // LRU_BLOCK):
        xb = xc[:, n * LRU_BLOCK:(n + 1) * LRU_BLOCK].astype(BF16)
        ra.append(jnp.dot(xb, wa_ref[n].astype(BF16), preferred_element_type=F32))
        gi.append(jnp.dot(xb, wx_ref[n].astype(BF16), preferred_element_type=F32))
    r = jax.nn.sigmoid(jnp.concatenate(ra, axis=1) + ba_ref[...])
    gin = jax.nn.sigmoid(jnp.concatenate(gi, axis=1) + bx_ref[...])
    z = -lam_ref[...]
    softplus = jnp.maximum(z, 0.0) + jnp.log1p(jnp.exp(-jnp.abs(z)))
    log_a = -LRU_C * r * softplus
    a_cum = jnp.exp(log_a)
    one_minus_a2 = -jnp.tanh(log_a) * (jnp.exp(2.0 * log_a) + 1.0)
    b_cum = jnp.sqrt(one_minus_a2) * (gin * xc)

    rows = lax.broadcasted_iota(I32, a_cum.shape, 0)
    d = 1
    while d < tc:
        keep = rows >= d
        a_sh = jnp.where(keep, pltpu.roll(a_cum, d, axis=0), 1.0)
        b_sh = jnp.where(keep, pltpu.roll(b_cum, d, axis=0), 0.0)
        b_cum = a_cum * b_sh + b_cum
        a_cum = a_sh * a_cum
        d *= 2
    h = a_cum * h_scr[...] + b_cum
    y_ref[...] = (h * jax.nn.gelu(gr_ref[...])).astype(y_ref.dtype)
    h_scr[...] = h[tc - 1:tc]
    prev_scr[...] = x[tc - SUBLANES:tc]

    @pl.when(t == pl.num_programs(2) - 1)
    def _():
        hl_ref[0] = h[tc - 1:tc]


def _rglru(proj, *, row0, n_batch, seq, tc, hist8, h0, conv_w, conv_b, wa, ba, wx, bx, lam, buf, name):
    n_tc = seq // tc
    rb0 = row0 // tc
    n_cb = D_RNN // LRU_COLS
    nb = LRU_COLS // LRU_BLOCK
    row_blk = lambda off: pl.BlockSpec((tc, LRU_COLS), lambda b, c, t, off=off: (rb0 + b * n_tc + t, off + c))
    vec = lambda: pl.BlockSpec((1, LRU_COLS), lambda b, c, t: (0, c))
    in_specs = [
        row_blk(0), row_blk(n_cb),
        pl.BlockSpec((1, SUBLANES, LRU_COLS), lambda b, c, t: (b, 0, c)),
        pl.BlockSpec((1, 1, LRU_COLS), lambda b, c, t: (b, 0, c)),
        pl.BlockSpec((CONV_W, LRU_COLS), lambda b, c, t: (0, c)), vec(),
        pl.BlockSpec((nb, LRU_BLOCK, LRU_BLOCK), lambda b, c, t: (c, 0, 0)), vec(),
        pl.BlockSpec((nb, LRU_BLOCK, LRU_BLOCK), lambda b, c, t: (c, 0, 0)), vec(), vec()]
    args = [proj, proj, hist8, h0.reshape(n_batch, 1, D_RNN), conv_w, conv_b.reshape(1, D_RNN),
            wa, ba.reshape(1, D_RNN), wx, bx.reshape(1, D_RNN), lam.reshape(1, D_RNN)]
    aliases = {}
    if buf is not None:
        in_specs.append(pl.BlockSpec(memory_space=pl.ANY))
        args.append(buf)
        aliases = {len(args) - 1: 0}
        y_shape = jax.ShapeDtypeStruct(buf.shape, buf.dtype)
        y_spec = pl.BlockSpec((tc, LRU_COLS), lambda b, c, t: (rb0 + b * n_tc + t, c))
    else:
        y_shape = jax.ShapeDtypeStruct((n_batch * seq, D_RNN), F32)
        y_spec = pl.BlockSpec((tc, LRU_COLS), lambda b, c, t: (b * n_tc + t, c))
    return pl.pallas_call(
        functools.partial(_rglru_body, tc=tc, aliased=buf is not None),
        grid=(n_batch, n_cb, n_tc),
        in_specs=in_specs,
        out_specs=[y_spec, pl.BlockSpec((1, 1, LRU_COLS), lambda b, c, t: (b, 0, c))],
        out_shape=[y_shape, jax.ShapeDtypeStruct((n_batch, 1, D_RNN), F32)],
        scratch_shapes=[pltpu.VMEM((SUBLANES, LRU_COLS), F32), pltpu.VMEM((1, LRU_COLS), F32)],
        input_output_aliases=aliases,
        compiler_params=_cparams(3), name=name,
    )(*args)


def _diff_lambda(lamv_ref, lam_init):
    lv = lamv_ref[...]
    s1 = jnp.sum(lv[0:1] * lv[1:2], axis=-1, keepdims=True)
    s2 = jnp.sum(lv[2:3] * lv[3:4], axis=-1, keepdims=True)
    return jnp.exp(s1) - jnp.exp(s2) + lam_init


ATT_TILE = 512


def _flash_body(qt_ref, k_ref, vt_ref, lamv_ref, sg_ref, buf_ref, o_ref, m_scr, l_scr, acc_scr, *, lam_init):
    i = pl.program_id(2)
    j = pl.program_id(3)

    @pl.when(j == 0)
    def _():
        m_scr[...] = jnp.full(m_scr.shape, NEG_INF, F32)
        l_scr[...] = jnp.zeros(l_scr.shape, F32)
        acc_scr[...] = jnp.zeros(acc_scr.shape, F32)

    def update(masked):
        vt = vt_ref[...]
        for c in range(2):
            hd = slice(c * HEAD_DIM, (c + 1) * HEAD_DIM)
            st = jnp.dot(k_ref[:, hd], qt_ref[hd, :], preferred_element_type=F32) * ATT_SCALE
            if masked:
                key = lax.broadcasted_iota(I32, st.shape, 0)
                query = lax.broadcasted_iota(I32, st.shape, 1)
                st = jnp.where(key <= query, st, NEG_INF)
            m_old = m_scr[c]
            m_new = jnp.maximum(m_old, jnp.max(st, axis=0, keepdims=True))
            alpha = jnp.exp(m_old - m_new)
            p = jnp.exp(st - m_new)
            l_scr[c] = alpha * l_scr[c] + jnp.sum(p, axis=0, keepdims=True)
            acc_scr[c] = alpha * acc_scr[c] + jnp.dot(vt, p.astype(BF16), preferred_element_type=F32)
            m_scr[c] = m_new

    @pl.when(j < i)
    def _():
        update(False)

    @pl.when(j == i)
    def _():
        update(True)
        lam = _diff_lambda(lamv_ref, lam_init)
        d = acc_scr[0] / l_scr[0] - lam * (acc_scr[1] / l_scr[1])
        inv = lax.rsqrt(jnp.mean(d * d, axis=0, keepdims=True) + EPS)
        o = ((d * inv) * sg_ref[...]) * (1.0 - lam_init)
        o_ref[...] = o.T.astype(o_ref.dtype)


def _flash(q_t, k_bf, v_t, lamv, subln_g, buf, lam_init):
    n_qt = SEQ // ATT_TILE
    q_spec = pl.BlockSpec((V_DIM, ATT_TILE), lambda b, h, i, j: (h, b * n_qt + i))
    k_spec = pl.BlockSpec((ATT_TILE, V_DIM), lambda b, h, i, j: (b * n_qt + jnp.minimum(i, j), h))
    v_spec = pl.BlockSpec((V_DIM, ATT_TILE), lambda b, h, i, j: (h, b * n_qt + jnp.minimum(i, j)))
    return pl.pallas_call(
        functools.partial(_flash_body, lam_init=lam_init),
        grid=(BATCH, N_HEADS, n_qt, n_qt),
        in_specs=[q_spec, k_spec, v_spec,
                  pl.BlockSpec((4, HEAD_DIM), lambda b, h, i, j: (0, 0)),
                  pl.BlockSpec((V_DIM, 1), lambda b, h, i, j: (0, 0)),
                  pl.BlockSpec(memory_space=pl.ANY)],
        out_specs=pl.BlockSpec((ATT_TILE, V_DIM), lambda b, h, i, j: (b * n_qt + i, D_RNN // V_DIM + h)),
        out_shape=jax.ShapeDtypeStruct(buf.shape, buf.dtype),
        scratch_shapes=[pltpu.VMEM((2, 1, ATT_TILE), F32), pltpu.VMEM((2, 1, ATT_TILE), F32),
                        pltpu.VMEM((2, V_DIM, ATT_TILE), F32)],
        input_output_aliases={5: 0},
        compiler_params=_cparams(4), name="flash_prompt",
    )(q_t, k_bf, v_t, lamv, subln_g.reshape(V_DIM, 1), buf)


N_HM = 2 * N_HEADS
Q_COLS = DEC_SEQ * N_HM


PAGES_PER_STEP = 4


def _paged_body(pt_ref, qt_ref, *refs, lam_init, n_steps):
    k_refs = refs[:PAGES_PER_STEP]
    v_refs = refs[PAGES_PER_STEP:2 * PAGES_PER_STEP]
    (kn_ref, vn_ref, lamv_ref, sg_ref, o_ref,
     m_scr, l_scr, acc_scr, bias_scr, p_scr) = refs[2 * PAGES_PER_STEP:]
    p = pl.program_id(1)
    k_rows = PAGE_SIZE * N_HM

    @pl.when(p == 0)
    def _():
        m_scr[...] = jnp.full(m_scr.shape, NEG_INF, F32)
        l_scr[...] = jnp.zeros(l_scr.shape, F32)
        acc_scr[...] = jnp.zeros(acc_scr.shape, F32)
        row = lax.broadcasted_iota(I32, (k_rows, Q_COLS), 0)
        col = lax.broadcasted_iota(I32, (k_rows, Q_COLS), 1)
        same_hm = jnp.bitwise_and(row, N_HM - 1) == jnp.bitwise_and(col, N_HM - 1)
        bias_scr[...] = jnp.where(same_hm, 0.0, NEG_INF)

    qt = qt_ref[0]

    def attend(blocks):
        scores = [jnp.dot(k.astype(BF16), qt, preferred_element_type=F32) * ATT_SCALE + bias
                  for k, _, bias in blocks]
        m_old = m_scr[...]
        m_new = m_old
        for st in scores:
            m_new = jnp.maximum(m_new, jnp.max(st, axis=0, keepdims=True))
        alpha = jnp.exp(m_old - m_new)
        l_new = alpha * l_scr[...]
        acc = alpha * acc_scr[...]
        for i, (st, (_, v, _)) in enumerate(zip(scores, blocks)):
            n = st.shape[0]
            pr = jnp.exp(st - m_new)
            l_new = l_new + jnp.sum(pr, axis=0, keepdims=True)
            p_scr[i, 0:n, :] = pr
            pair = p_scr[i, pl.ds(0, n // 2, stride=2), :] + p_scr[i, pl.ds(1, n // 2, stride=2), :]
            acc = acc + lax.dot_general(v.astype(BF16), pair.astype(BF16), (((0,), (0,)), ((), ())),
                                        preferred_element_type=F32)
        l_scr[...] = l_new
        m_scr[...] = m_new
        acc_scr[...] = acc

    attend([(k_ref[...], v_ref[...], bias_scr[...]) for k_ref, v_ref in zip(k_refs, v_refs)])

    @pl.when(p == n_steps - 1)
    def _():
        n_new = DEC_SEQ * N_HM
        row = lax.broadcasted_iota(I32, (n_new, Q_COLS), 0)
        col = lax.broadcasted_iota(I32, (n_new, Q_COLS), 1)
        shift = int(math.log2(N_HM))
        keep = jnp.logical_and(jnp.bitwise_and(row, N_HM - 1) == jnp.bitwise_and(col, N_HM - 1),
                               jnp.right_shift(row, shift) <= jnp.right_shift(col, shift))
        attend([(kn_ref[0], vn_ref[0], jnp.where(keep, 0.0, NEG_INF))])
        lam = _diff_lambda(lamv_ref, lam_init)
        on = acc_scr[...] / l_scr[...]
        d = on - lam * pltpu.roll(on, Q_COLS - 1, axis=1)
        inv = lax.rsqrt(jnp.mean(d * d, axis=0, keepdims=True) + EPS)
        o_ref[0] = ((d * inv) * sg_ref[...]) * (1.0 - lam_init)


def _paged(qt_s, cache_k, cache_v, layer, page_table, k_new, v_new, lamv, subln_g, lam_init):
    n_pages = page_table.shape[1]
    assert n_pages % PAGES_PER_STEP == 0
    n_steps = n_pages // PAGES_PER_STEP
    k_rows, v_rows = cache_k.shape[2], cache_v.shape[2]

    def page_spec(rows, width, i):
        return pl.BlockSpec((None, None, rows, width),
                            lambda b, p, pt, i=i: (layer, pt[b * n_pages + p * PAGES_PER_STEP + i], 0, 0))

    per_batch = lambda shape: pl.BlockSpec((1,) + shape, lambda b, p, pt: (b, 0, 0))
    return pl.pallas_call(
        functools.partial(_paged_body, lam_init=lam_init, n_steps=n_steps),
        grid_spec=pltpu.PrefetchScalarGridSpec(
            num_scalar_prefetch=1,
            grid=(DEC_BATCH, n_steps),
            in_specs=[per_batch((HEAD_DIM, Q_COLS))]
            + [page_spec(k_rows, HEAD_DIM, i) for i in range(PAGES_PER_STEP)]
            + [page_spec(v_rows, V_DIM, i) for i in range(PAGES_PER_STEP)]
            + [per_batch((DEC_SEQ * N_HM, HEAD_DIM)), per_batch((DEC_SEQ * N_HEADS, V_DIM)),
               pl.BlockSpec((4, HEAD_DIM), lambda b, p, pt: (0, 0)),
               pl.BlockSpec((V_DIM, 1), lambda b, p, pt: (0, 0))],
            out_specs=per_batch((V_DIM, Q_COLS)),
            scratch_shapes=[pltpu.VMEM((1, Q_COLS), F32), pltpu.VMEM((1, Q_COLS), F32),
                            pltpu.VMEM((V_DIM, Q_COLS), F32),
                            pltpu.VMEM((k_rows, Q_COLS), F32),
                            pltpu.VMEM((PAGES_PER_STEP, k_rows, Q_COLS), F32)]),
        out_shape=jax.ShapeDtypeStruct((DEC_BATCH, V_DIM, Q_COLS), F32),
        compiler_params=_cparams(2), name="paged_sample",
    )(page_table.reshape(-1), qt_s, *([cache_k] * PAGES_PER_STEP), *([cache_v] * PAGES_PER_STEP),
      k_new, v_new, lamv, subln_g.reshape(V_DIM, 1))


def _place_body(rows_ref, buf_ref, o_ref):
    o_ref[...] = rows_ref[...]


def _place_rows(buf, rows, row0):
    n = rows.shape[0]
    assert row0 % n == 0
    return pl.pallas_call(
        _place_body,
        grid=(1,),
        in_specs=[pl.BlockSpec(rows.shape, lambda i: (0, 0)), pl.BlockSpec(memory_space=pl.ANY)],
        out_specs=pl.BlockSpec(rows.shape, lambda i: (row0 // n, 0)),
        out_shape=jax.ShapeDtypeStruct(buf.shape, buf.dtype),
        input_output_aliases={1: 0},
        compiler_params=_cparams(1), name="place_rows",
    )(rows, buf)


N_ASSIGN = TOP_K * M_REAL
MOE_TILES = (N_ASSIGN + N_EXPERTS * (TOKEN_TILE - 1)) // TOKEN_TILE
MOE_ROWS = MOE_TILES * TOKEN_TILE
GATHER_TILE = 256
COMBINE_TILE = 128


def _row_copy(src_hbm, row, dst_vmem, slot, sem):
    return pltpu.make_async_copy(src_hbm.at[pl.ds(row, 1), :], dst_vmem.at[pl.ds(slot, 1), :], sem)


DMA_ISSUE_UNROLL = 8


def _gather_body(src_ref, x_hbm, o_ref, buf, sem):
    def start(g, carry):
        for u in range(DMA_ISSUE_UNROLL):
            r = g * DMA_ISSUE_UNROLL + u
            _row_copy(x_hbm, src_ref[0, r], buf, r, sem).start(priority=u % 2)
        return carry

    lax.fori_loop(0, GATHER_TILE // DMA_ISSUE_UNROLL, start, 0)
    for r in range(GATHER_TILE):
        _row_copy(x_hbm, 0, buf, r, sem).wait()
    o_ref[...] = buf[...].astype(BF16)


def _gather_rows(src_rows, x32):
    n_t = MOE_ROWS // GATHER_TILE
    return pl.pallas_call(
        _gather_body,
        grid=(n_t,),
        in_specs=[pl.BlockSpec((None, 1, GATHER_TILE), lambda i: (i, 0, 0), memory_space=pltpu.SMEM),
                  pl.BlockSpec(memory_space=pl.ANY)],
        out_specs=pl.BlockSpec((GATHER_TILE, D_MODEL), lambda i: (i, 0)),
        out_shape=jax.ShapeDtypeStruct((MOE_ROWS, D_MODEL), BF16),
        scratch_shapes=[pltpu.VMEM((GATHER_TILE, D_MODEL), F32), pltpu.SemaphoreType.DMA(())],
        compiler_params=_cparams(1), name="moe_gather",
    )(src_rows.reshape(n_t, 1, GATHER_TILE), x32)


def _combine_body(pos_ref, ys_hbm, g_ref, o_ref, buf, sem):
    def start(g, carry):
        for u in range(DMA_ISSUE_UNROLL):
            r = g * DMA_ISSUE_UNROLL + u
            for k in range(TOP_K):
                _row_copy(ys_hbm, pos_ref[k, r], buf.at[k], r, sem).start(priority=k)
        return carry

    lax.fori_loop(0, COMBINE_TILE // DMA_ISSUE_UNROLL, start, 0)
    for r in range(COMBINE_TILE):
        for k in range(TOP_K):
            _row_copy(ys_hbm, 0, buf.at[k], r, sem).wait()
    g = g_ref[...]
    o_ref[...] = g[:, 0:1] * buf[0] + g[:, 1:2] * buf[1]


def _combine_rows(pos, gates_t, ys):
    n_t = M_PAD // COMBINE_TILE
    pos_blk = pos.reshape(TOP_K, n_t, COMBINE_TILE).transpose(1, 0, 2)
    return pl.pallas_call(
        _combine_body,
        grid=(n_t,),
        in_specs=[pl.BlockSpec((None, TOP_K, COMBINE_TILE), lambda i: (i, 0, 0), memory_space=pltpu.SMEM),
                  pl.BlockSpec(memory_space=pl.ANY),
                  pl.BlockSpec((COMBINE_TILE, TOP_K), lambda i: (i, 0))],
        out_specs=pl.BlockSpec((COMBINE_TILE, D_MODEL), lambda i: (i, 0)),
        out_shape=jax.ShapeDtypeStruct((M_PAD, D_MODEL), F32),
        scratch_shapes=[pltpu.VMEM((TOP_K, COMBINE_TILE, D_MODEL), F32), pltpu.SemaphoreType.DMA(())],
        compiler_params=_cparams(1), name="moe_combine",
    )(pos_blk, ys, gates_t)


def _moe_plan(idx):
    e_a = idx[:, :M_REAL].reshape(-1)
    onehot = (e_a[:, None] == jnp.arange(N_EXPERTS, dtype=I32)[None, :]).astype(I32)
    running = jnp.cumsum(onehot, axis=0).astype(I32)
    counts = running[-1]
    rank = jnp.sum((running - onehot) * onehot, axis=1)
    tiles_e = (counts + TOKEN_TILE - 1) // TOKEN_TILE
    tile_end = jnp.cumsum(tiles_e).astype(I32)
    tile_start = tile_end - tiles_e
    dest = tile_start[e_a] * TOKEN_TILE + rank
    pos = jnp.pad(dest.reshape(TOP_K, M_REAL), ((0, 0), (0, M_PAD - M_REAL)))
    tl = jnp.arange(MOE_TILES, dtype=I32)
    tl_used = jnp.minimum(tl, tile_end[-1] - 1)
    tile_expert = jnp.minimum(jnp.searchsorted(tile_end, tl_used, side="right").astype(I32), N_EXPERTS - 1)
    tile_valid = (tl < tile_end[-1]).astype(I32)
    tile_first = jnp.logical_and(tile_valid > 0, tl == tile_start[tile_expert]).astype(I32)
    order = jnp.argsort(e_a, stable=True).astype(I32)
    first_sorted = (jnp.cumsum(counts) - counts).astype(I32)
    row = jnp.arange(MOE_ROWS, dtype=I32)
    row_tile = row // TOKEN_TILE
    row_expert = tile_expert[row_tile]
    offset = row - tile_start[row_expert] * TOKEN_TILE
    used = jnp.logical_and(tile_valid[row_tile] > 0, offset < counts[row_expert])
    assignment = order[jnp.clip(first_sorted[row_expert] + offset, 0, N_ASSIGN - 1)]
    src_rows = jnp.where(used, assignment % M_REAL, 0).astype(I32)
    return src_rows, pos, tile_valid, tile_expert, tile_first


PROJ_TN, PROJ_KC = 1024, 1024
GU_TN, GU_KC = 512, 1024
DOWN_TM, DOWN_TN, DOWN_KC = 256, 512, 2048


def _ffn(h_rows, wg, wu, wd, sched_gu, sched_down, tag):
    hmid = _matmul(h_rows, [wg, wu], sched_gu, tm=TOKEN_TILE, tn=GU_TN, kc=GU_KC, out_dtype=BF16,
                   swiglu=True, name=tag + "_gate_up")
    return _matmul(hmid, [wd], sched_down, tm=DOWN_TM, tn=DOWN_TN, kc=DOWN_KC, out_dtype=F32,
                   name=tag + "_down")


def _split_tiles(valid, expert, first, parts):
    sub_first = jnp.stack([first] + [jnp.zeros_like(first)] * (parts - 1), axis=1).reshape(-1)
    return jnp.repeat(valid, parts), jnp.repeat(expert, parts), sub_first


def kernel(x_prompt, x_sample, c_prompt, c_sample, cache_k, cache_v, page_table, state_h, state_conv, w_in, conv_w, conv_b, lru_wa, lru_ba, lru_wx, lru_bx, lru_lambda, lambda_q1, lambda_k1, lambda_q2, lambda_k2, subln_g, w_out, w_mod, b_mod, g_pre_mix, g_post_mix, g_pre_ffn, g_post_ffn, w_gate, w_up, w_down, router_w, router_b, we_gate, we_up, we_down):
    n_pages = page_table.shape[1]
    past_len = n_pages * cache_k.shape[2]

    x = jnp.concatenate([x_prompt.reshape(M_PROMPT, D_MODEL), x_sample.reshape(M_SAMPLE, D_MODEL),
                         jnp.zeros((M_PAD - M_REAL, D_MODEL), F32)], axis=0)
    c_rows = jnp.concatenate([c_prompt, c_sample,
                              jnp.zeros((N_MOD_ROWS - BATCH - DEC_BATCH, D_MODEL), F32)], axis=0)

    pos_rows = np.concatenate([np.tile(np.arange(SEQ), BATCH),
                               np.tile(past_len + np.arange(DEC_SEQ), DEC_BATCH),
                               np.zeros((M_PAD - M_REAL,))]).astype(np.float32)
    half = HEAD_DIM // 2
    inv_freq = ROPE_THETA ** (-jnp.arange(half, dtype=F32) / half)
    ang = jnp.asarray(pos_rows)[:, None] * inv_freq[None, :]
    cos_tab = jnp.concatenate([jnp.cos(ang), jnp.cos(ang)], axis=1)
    sin_tab = jnp.concatenate([-jnp.sin(ang), jnp.sin(ang)], axis=1)

    n_tok_tiles = M_PAD // TOKEN_TILE
    sched_k4096 = [_dense_schedule(n_tok_tiles, D_MODEL // PROJ_KC, l) for l in range(DEPTH)]
    sched_gu = [_dense_schedule(n_tok_tiles, D_MODEL // GU_KC, i) for i in range(w_gate.shape[0])]
    sched_down = [_dense_schedule(M_PAD // DOWN_TM, D_FF // DOWN_KC, i) for i in range(w_down.shape[0])]
    sched_mod = [_dense_schedule(1, D_MODEL // PROJ_KC, l) for l in range(DEPTH)]

    mods = [_matmul(c_rows, [w_mod], sched_mod[l], tm=N_MOD_ROWS, tn=PROJ_TN, kc=PROJ_KC,
                    out_dtype=F32, bias=b_mod[l], x_silu=True, name=f"mod{l}") for l in range(DEPTH)]
    moe_w = [w.reshape((-1,) + w.shape[2:]) for w in (we_gate, we_up, we_down)]

    hm = _norm(x, mods[0], g_next=g_pre_mix[0], shift_col=0, scale_col=1, name="pre_mix0")[0]

    zeros_hist = jnp.zeros((BATCH, SUBLANES, D_RNN), F32)
    zeros_h0 = jnp.zeros((BATCH, D_RNN), F32)
    cache_k2 = cache_k.reshape(cache_k.shape[0], cache_k.shape[1], -1, HEAD_DIM)
    cache_v2 = cache_v.reshape(cache_v.shape[0], cache_v.shape[1], -1, V_DIM)

    ks, vs, hs, cs = [], [], [], []
    for l in range(DEPTH):
        mod = mods[l]
        lam_init = 0.8 - 0.6 * math.exp(-0.3 * l)
        proj = _matmul(hm, [w_in], sched_k4096[l], tm=TOKEN_TILE, tn=PROJ_TN, kc=PROJ_KC,
                       out_dtype=F32, name=f"w_in{l}")
        q_bf, k_f32, k_bf, v_bf = _rope(proj, cos_tab, sin_tab)

        lru = dict(conv_w=conv_w[l], conv_b=conv_b[l], wa=lru_wa[l], ba=lru_ba[l], wx=lru_wx[l],
                   bx=lru_bx[l], lam=lru_lambda[l])
        buf = jnp.zeros((M_PAD, D_MODEL), BF16)
        buf, h_last_p = _rglru(proj, row0=0, n_batch=BATCH, seq=SEQ, tc=256, hist8=zeros_hist,
                               h0=zeros_h0, buf=buf, name=f"rglru_prompt{l}", **lru)
        hist_s = jnp.pad(state_conv[l], ((0, 0), (SUBLANES - (CONV_W - 1), 0), (0, 0)))
        y_s, h_last_s = _rglru(proj, row0=M_PROMPT, n_batch=DEC_BATCH, seq=DEC_SEQ, tc=DEC_SEQ,
                               hist8=hist_s, h0=state_h[l], buf=None, name=f"rglru_sample{l}", **lru)

        lamv = jnp.stack([lambda_q1[l], lambda_k1[l], lambda_q2[l], lambda_k2[l]], axis=0)
        buf = _flash(q_bf.T, k_bf, v_bf.T, lamv, subln_g[l], buf, lam_init)
        q_s = q_bf[M_PROMPT:M_REAL].reshape(DEC_BATCH, Q_COLS, HEAD_DIM)
        k_s = k_f32[M_PROMPT:M_REAL].reshape(DEC_BATCH, DEC_SEQ * N_HM, HEAD_DIM)
        v_s = proj[M_PROMPT:M_REAL, 2 * D_RNN + 2 * D_QK:].reshape(DEC_BATCH, DEC_SEQ * N_HEADS, V_DIM)
        o_t = _paged(jnp.swapaxes(q_s, 1, 2), cache_k2, cache_v2, l, page_table, k_s, v_s,
                     lamv, subln_g[l], lam_init)
        o_s = jnp.transpose(o_t.reshape(DEC_BATCH, V_DIM, DEC_SEQ, N_HEADS, 2)[..., 0], (0, 2, 3, 1))
        mix_s = jnp.concatenate([y_s, o_s.reshape(M_SAMPLE, D_ATT)], axis=1).astype(BF16)
        buf = _place_rows(buf, mix_s, M_PROMPT)

        m = _matmul(buf, [w_out], sched_k4096[l], tm=TOKEN_TILE, tn=PROJ_TN, kc=PROJ_KC,
                    out_dtype=F32, name=f"w_out{l}")
        i = l // 2
        if l % 2 == 0:
            x, hf = _norm(x, mod, f=m, g_post=g_post_mix[l], gate_col=2, g_next=g_pre_ffn[l],
                          shift_col=3, scale_col=4, name=f"post_mix{l}")
            f = _ffn(hf, w_gate, w_up, w_down, sched_gu[i], sched_down[i], f"ffn{l}")
        else:
            x, hf, hf32, idx, gates = _norm(x, mod, f=m, g_post=g_post_mix[l], gate_col=2,
                                            g_next=g_pre_ffn[l], shift_col=3, scale_col=4,
                                            router_w=router_w[i], router_b=router_b[i],
                                            name=f"post_mix{l}")
            src_rows, pos, t_valid, t_expert, t_first = _moe_plan(idx)
            t_expert = t_expert + i * N_EXPERTS
            xs = _gather_rows(src_rows, hf32)
            nkc_gu, nkc_dn = D_MODEL // GU_KC, D_FF // DOWN_KC
            moe_gu = _build_schedule(t_valid, t_expert, t_first, nkc_gu,
                                     MOE_TILES + N_EXPERTS * (nkc_gu - 1))
            parts = TOKEN_TILE // DOWN_TM
            moe_dn = _build_schedule(*_split_tiles(t_valid, t_expert, t_first, parts), nkc_dn,
                                     MOE_TILES * parts + N_EXPERTS * (nkc_dn - 1))
            ys = _ffn(xs, moe_w[0], moe_w[1], moe_w[2], moe_gu, moe_dn, f"moe{l}")
            gates_t = jnp.where(jnp.arange(M_PAD)[:, None] < M_REAL, gates.T, 0.0)
            f = _combine_rows(pos, gates_t, ys)
        if l + 1 < DEPTH:
            x, hm = _norm(x, mod, f=f, g_post=g_post_ffn[l], gate_col=5, g_next=g_pre_mix[l + 1],
                          shift_col=0, scale_col=1, mod_next=mods[l + 1], name=f"post_ffn{l}")
        else:
            x = _norm(x, mod, f=f, g_post=g_post_ffn[l], gate_col=5, name=f"post_ffn{l}")[0]

        ks.append(k_f32)
        vs.append(proj[:, 2 * D_RNN + 2 * D_QK:])
        hs.append((h_last_p, h_last_s))
        tail = CONV_W - 1
        cs.append((jnp.stack([proj[(b + 1) * SEQ - tail:(b + 1) * SEQ, :D_RNN] for b in range(BATCH)]),
                   jnp.stack([proj[M_PROMPT + (b + 1) * DEC_SEQ - tail:M_PROMPT + (b + 1) * DEC_SEQ, :D_RNN]
                              for b in range(DEC_BATCH)])))

    def prompt_rows(a):
        return a[:M_PROMPT]

    def sample_rows(a):
        return a[M_PROMPT:M_REAL]

    y_prompt = prompt_rows(x).reshape(BATCH, SEQ, D_MODEL)
    y_sample = sample_rows(x).reshape(DEC_BATCH, DEC_SEQ, D_MODEL)
    k_prompt = jnp.stack([prompt_rows(k).reshape(BATCH, SEQ, N_HEADS, 2, HEAD_DIM) for k in ks])
    k_sample = jnp.stack([sample_rows(k).reshape(DEC_BATCH, DEC_SEQ, N_HEADS, 2, HEAD_DIM) for k in ks])
    v_prompt = jnp.stack([prompt_rows(v).reshape(BATCH, SEQ, N_HEADS, V_DIM) for v in vs])
    v_sample = jnp.stack([sample_rows(v).reshape(DEC_BATCH, DEC_SEQ, N_HEADS, V_DIM) for v in vs])
    h_prompt = jnp.stack([h[0].reshape(BATCH, D_RNN) for h in hs])
    h_sample = jnp.stack([h[1].reshape(DEC_BATCH, D_RNN) for h in hs])
    conv_prompt = jnp.stack([c[0] for c in cs])
    conv_sample = jnp.stack([c[1] for c in cs])
    return (y_prompt, y_sample, k_prompt, v_prompt, h_prompt, conv_prompt,
            k_sample, v_sample, h_sample, conv_sample)
```

```python
import functools
import math

import numpy as np
import jax
import jax.numpy as jnp
from jax import lax
from jax.experimental import pallas as pl
from jax.experimental.pallas import tpu as pltpu

F32 = jnp.float32
BF16 = jnp.bfloat16
I32 = jnp.int32

D_MODEL = 4096
BATCH = 4
SEQ = 2048
DEPTH = 2
DEC_BATCH = 8
DEC_SEQ = 8
PAGE_SIZE = 128
D_RNN = D_MODEL // 2
N_LRU_BLOCKS = 16
LRU_BLOCK = D_RNN // N_LRU_BLOCKS
CONV_W = 4
LRU_C = 8.0
D_ATT = D_MODEL - D_RNN
HEAD_DIM = 128
N_HEADS = D_ATT // (2 * HEAD_DIM)
V_DIM = 2 * HEAD_DIM
D_QK = N_HEADS * 2 * HEAD_DIM
D_IN = 2 * D_RNN + 2 * D_QK + N_HEADS * V_DIM
ROPE_THETA = 10000.0
D_FF = (D_MODEL * 7) // 2
N_EXPERTS = 8
TOP_K = 2
EPS = 1e-6
NEG_INF = -1e30
ATT_SCALE = HEAD_DIM ** -0.5

M_PROMPT = BATCH * SEQ
M_SAMPLE = DEC_BATCH * DEC_SEQ
M_REAL = M_PROMPT + M_SAMPLE
TOKEN_TILE = 512
M_PAD = -(-M_REAL // TOKEN_TILE) * TOKEN_TILE
N_MOD_ROWS = 16

LANES = 128
SUBLANES = 8
VMEM_LIMIT_BYTES = 56 * 1024 * 1024


def _cparams(n_grid_dims):
    return pltpu.CompilerParams(
        dimension_semantics=("arbitrary",) * n_grid_dims,
        vmem_limit_bytes=VMEM_LIMIT_BYTES)


def _build_schedule(tile_valid, tile_expert, tile_first, nkc, n_steps):
    n_tiles = tile_valid.shape[0]
    spt = jnp.where(jnp.logical_and(tile_valid > 0, tile_first > 0), nkc, 1).astype(I32)
    ends = jnp.cumsum(spt).astype(I32)
    starts = ends - spt
    total = ends[-1]
    s = jnp.arange(n_steps, dtype=I32)
    s_eff = jnp.minimum(s, total - 1)
    t = jnp.minimum(jnp.searchsorted(ends, s_eff, side="right").astype(I32), n_tiles - 1)
    valid = tile_valid[t] > 0
    first = jnp.logical_and(valid, tile_first[t] > 0)
    chunk = jnp.where(first, s_eff - starts[t], nkc - 1).astype(I32)
    kind = jnp.where(s < total, jnp.where(valid, jnp.where(first, 1, 2), 3), 0).astype(I32)
    return kind, t, tile_expert[t].astype(I32), chunk


def _dense_schedule(n_tiles, nkc, expert=0):
    valid = np.ones((n_tiles,), np.int32)
    first = np.zeros((n_tiles,), np.int32)
    first[0] = 1
    return _build_schedule(jnp.asarray(valid), jnp.full((n_tiles,), expert, I32), jnp.asarray(first),
                           nkc, n_tiles + nkc - 1)


def _silu_mul(g, u):
    return jax.nn.silu(g) * u


def _mm_body(kind_ref, tile_ref, exp_ref, chunk_ref, x_ref, *refs, n_w, nkc, kc, swiglu, has_bias,
             x_silu):
    w_refs = refs[:n_w]
    pos = n_w
    bias_ref = None
    if has_bias:
        bias_ref = refs[pos]
        pos += 1
    out_ref = refs[pos]
    wb_refs = refs[pos + 1:pos + 1 + n_w]
    acc_refs = refs[pos + 1 + n_w:pos + 1 + 2 * n_w]
    s = pl.program_id(1)
    kind = kind_ref[s]
    chunk = chunk_ref[s]

    def load_x(lo, hi):
        xs = x_ref[:, lo:hi]
        if x_silu:
            xs = jax.nn.silu(xs).astype(BF16)
        return xs

    def epilogue(vals):
        r = _silu_mul(vals[0], vals[1]) if swiglu else vals[0]
        if has_bias:
            r = r + bias_ref[...]
        return r.astype(out_ref.dtype)

    for ci in range(nkc):
        @pl.when(jnp.logical_and(kind == 1, chunk == ci))
        def _(ci=ci):
            lo = ci * kc
            xs = load_x(lo, lo + kc)
            for w_ref, wb_ref, acc_ref in zip(w_refs, wb_refs, acc_refs):
                wc = w_ref[...].astype(BF16)
                wb_ref[lo:lo + kc, :] = wc
                part = jnp.dot(xs, wc, preferred_element_type=F32)
                if ci == 0:
                    acc_ref[...] = part
                else:
                    acc_ref[...] += part
            if ci == nkc - 1:
                out_ref[...] = epilogue([a[...] for a in acc_refs])

    @pl.when(kind == 2)
    def _():
        xs = load_x(0, nkc * kc)
        out_ref[...] = epilogue([jnp.dot(xs, wb[...], preferred_element_type=F32) for wb in wb_refs])

    @pl.when(kind == 3)
    def _():
        out_ref[...] = jnp.zeros(out_ref.shape, out_ref.dtype)


def _matmul(x, ws, sched, *, tm, tn, kc, out_dtype, swiglu=False, bias=None, x_silu=False, name):
    n_rows, k_dim = x.shape
    n_cols = ws[0].shape[2]
    n_w = len(ws)
    nkc = k_dim // kc
    assert nkc * kc == k_dim and n_cols % tn == 0 and n_rows % tm == 0
    n_steps = sched[0].shape[0]
    imap_x = lambda j, s, kind, tile, exp, chunk: (tile[s], 0)
    imap_w = lambda j, s, kind, tile, exp, chunk: (exp[s], chunk[s], j)
    imap_o = lambda j, s, kind, tile, exp, chunk: (tile[s], j)
    in_specs = [pl.BlockSpec((tm, k_dim), imap_x)]
    in_specs += [pl.BlockSpec((None, kc, tn), imap_w) for _ in ws]
    args = [x] + list(ws)
    if bias is not None:
        in_specs.append(pl.BlockSpec((1, tn), lambda j, s, kind, tile, exp, chunk: (0, j)))
        args.append(bias.reshape(1, n_cols))
    body = functools.partial(_mm_body, n_w=n_w, nkc=nkc, kc=kc, swiglu=swiglu,
                             has_bias=bias is not None, x_silu=x_silu)
    return pl.pallas_call(
        body,
        grid_spec=pltpu.PrefetchScalarGridSpec(
            num_scalar_prefetch=4,
            grid=(n_cols // tn, n_steps),
            in_specs=in_specs,
            out_specs=pl.BlockSpec((tm, tn), imap_o),
            scratch_shapes=[pltpu.VMEM((k_dim, tn), BF16) for _ in ws]
            + [pltpu.VMEM((tm, tn), F32) for _ in ws]),
        out_shape=jax.ShapeDtypeStruct((n_rows, n_cols), out_dtype),
        compiler_params=_cparams(2),
        name=name,
    )(*sched, *args)


NORM_TILE = 128


def _row_mod_onehot(tile_rows):
    row = pl.program_id(0) * tile_rows + lax.broadcasted_iota(I32, (tile_rows, 1), 0)
    prompt_id = jnp.right_shift(row, int(math.log2(SEQ)))
    sample_id = BATCH + jnp.right_shift(row - M_PROMPT, int(math.log2(DEC_SEQ)))
    bid = jnp.minimum(jnp.where(row < M_PROMPT, prompt_id, sample_id), N_MOD_ROWS - 1)
    return (bid == lax.broadcasted_iota(I32, (1, N_MOD_ROWS), 1)).astype(F32)


def _rms(v):
    return v * lax.rsqrt(jnp.mean(v * v, axis=-1, keepdims=True) + EPS)


def _select_rows(onehot, mod_ref):
    return jnp.dot(onehot, mod_ref[...], preferred_element_type=F32, precision=lax.Precision.HIGHEST)


def _norm_body(*refs, has_f, has_next, router):
    refs = list(refs)
    take = lambda n: [refs.pop(0) for _ in range(n)]
    (x_ref,) = take(1)
    if has_f:
        f_ref, gpost_ref, gate_row_ref, gate_all_ref = take(4)
    if has_next:
        gnext_ref, shift_row_ref, shift_all_ref, scale_row_ref, scale_all_ref = take(5)
    if router:
        rwt_ref, rb_ref = take(2)
    if has_f:
        (xnew_ref,) = take(1)
    if has_next:
        (h_ref,) = take(1)
    if router:
        h32_ref, idx_ref, gate_ref = take(3)

    def run(select):
        x = x_ref[...]
        if has_f:
            fn = _rms(f_ref[...]) * gpost_ref[...]
            x = x + select(gate_row_ref, gate_all_ref) * fn
            xnew_ref[...] = x
        if has_next:
            y = _rms(x) * gnext_ref[...]
            h = y * (1.0 + select(scale_row_ref, scale_all_ref)) + select(shift_row_ref, shift_all_ref)
            h_ref[...] = h.astype(BF16)
        if router:
            h32_ref[...] = h
            logits = lax.dot_general(rwt_ref[...], h, (((1,), (1,)), ((), ())),
                                     preferred_element_type=F32,
                                     precision=lax.Precision.HIGHEST) + rb_ref[...]
            eidx = lax.broadcasted_iota(I32, logits.shape, 0)
            m1 = jnp.max(logits, axis=0, keepdims=True)
            i1 = jnp.min(jnp.where(logits == m1, eidx, N_EXPERTS), axis=0, keepdims=True)
            rest = jnp.where(eidx == i1, -jnp.inf, logits)
            m2 = jnp.max(rest, axis=0, keepdims=True)
            i2 = jnp.min(jnp.where(rest == m2, eidx, N_EXPERTS), axis=0, keepdims=True)
            e2 = jnp.exp(m2 - m1)
            denom = 1.0 + e2
            idx_ref[...] = jnp.concatenate([i1, i2], axis=0)
            gate_ref[...] = jnp.concatenate([1.0 / denom, e2 / denom], axis=0)

    @pl.when(pl.program_id(0) < M_PROMPT // NORM_TILE)
    def _():
        run(lambda row_ref, all_ref: row_ref[...])

    @pl.when(pl.program_id(0) >= M_PROMPT // NORM_TILE)
    def _():
        onehot = _row_mod_onehot(NORM_TILE)
        run(lambda row_ref, all_ref: _select_rows(onehot, all_ref))


def _norm(x, mod, *, f=None, g_post=None, gate_col=None, g_next=None, shift_col=None, scale_col=None,
          router_w=None, router_b=None, mod_next=None, name):
    if mod_next is None:
        mod_next = mod
    has_f, has_next, router = f is not None, g_next is not None, router_w is not None
    tiles_per_seq = SEQ // NORM_TILE
    row_spec = pl.BlockSpec((NORM_TILE, D_MODEL), lambda i: (i, 0))
    vec_spec = pl.BlockSpec((1, D_MODEL), lambda i: (0, 0))

    def mod_views(m, col):
        row_view = (m.reshape(N_MOD_ROWS, 1, 6 * D_MODEL),
                    pl.BlockSpec((None, 1, D_MODEL),
                                 lambda i, col=col: (jnp.minimum(i // tiles_per_seq, BATCH - 1), 0, col)))
        all_view = (m, pl.BlockSpec((N_MOD_ROWS, D_MODEL), lambda i, col=col: (0, col)))
        return [row_view, all_view]

    operands = [(x, row_spec)]
    out_shape, out_specs = [], []
    if has_f:
        operands += [(f, row_spec), (g_post.reshape(1, D_MODEL), vec_spec)] + mod_views(mod, gate_col)
        out_shape.append(jax.ShapeDtypeStruct((M_PAD, D_MODEL), F32))
        out_specs.append(row_spec)
    if has_next:
        operands += ([(g_next.reshape(1, D_MODEL), vec_spec)] + mod_views(mod_next, shift_col)
                     + mod_views(mod_next, scale_col))
        out_shape.append(jax.ShapeDtypeStruct((M_PAD, D_MODEL), BF16))
        out_specs.append(row_spec)
    if router:
        operands += [(router_w.T, pl.BlockSpec((N_EXPERTS, D_MODEL), lambda i: (0, 0))),
                     (router_b.reshape(N_EXPERTS, 1), pl.BlockSpec((N_EXPERTS, 1), lambda i: (0, 0)))]
        out_shape += [jax.ShapeDtypeStruct((M_PAD, D_MODEL), F32),
                      jax.ShapeDtypeStruct((TOP_K, M_PAD), I32),
                      jax.ShapeDtypeStruct((TOP_K, M_PAD), F32)]
        out_specs += [row_spec, pl.BlockSpec((TOP_K, NORM_TILE), lambda i: (0, i)),
                      pl.BlockSpec((TOP_K, NORM_TILE), lambda i: (0, i))]
    return pl.pallas_call(
        functools.partial(_norm_body, has_f=has_f, has_next=has_next, router=router),
        grid=(M_PAD // NORM_TILE,),
        in_specs=[spec for _, spec in operands], out_specs=out_specs, out_shape=out_shape,
        compiler_params=_cparams(1), name=name,
    )(*[arr for arr, _ in operands])


ROPE_TILE = 256


def _rope_body(q_ref, k_ref, v_ref, cos_ref, sin_ref, qb_ref, kf_ref, kb_ref, vb_ref):
    cos = cos_ref[...]
    sin = sin_ref[...]

    def rot(xg):
        return xg * cos + pltpu.roll(xg, HEAD_DIM // 2, axis=1) * sin

    for g in range(D_QK // HEAD_DIM):
        cols = slice(g * HEAD_DIM, (g + 1) * HEAD_DIM)
        qb_ref[:, cols] = rot(q_ref[:, cols]).astype(BF16)
        kr = rot(k_ref[:, cols])
        kf_ref[:, cols] = kr
        kb_ref[:, cols] = kr.astype(BF16)
    vb_ref[...] = v_ref[...].astype(BF16)


def _rope(proj, cos_tab, sin_tab):
    blk = lambda col: pl.BlockSpec((ROPE_TILE, D_QK), lambda i, col=col: (i, col))
    tab = pl.BlockSpec((ROPE_TILE, HEAD_DIM), lambda i: (i, 0))
    q_col = 2 * D_RNN // D_QK
    return pl.pallas_call(
        _rope_body,
        grid=(M_PAD // ROPE_TILE,),
        in_specs=[blk(q_col), blk(q_col + 1), blk(q_col + 2), tab, tab],
        out_specs=[blk(0), blk(0), blk(0), blk(0)],
        out_shape=[jax.ShapeDtypeStruct((M_PAD, D_QK), BF16), jax.ShapeDtypeStruct((M_PAD, D_QK), F32),
                   jax.ShapeDtypeStruct((M_PAD, D_QK), BF16), jax.ShapeDtypeStruct((M_PAD, D_QK), BF16)],
        compiler_params=_cparams(1), name="rope",
    )(proj, proj, proj, cos_tab, sin_tab)


LRU_COLS = 512


def _rglru_body(xr_ref, gr_ref, hist_ref, h0_ref, cw_ref, cb_ref, wa_ref, ba_ref, wx_ref, bx_ref,
                lam_ref, *refs, tc, aliased):
    if aliased:
        refs = refs[1:]
    y_ref, hl_ref, prev_scr, h_scr = refs
    t = pl.program_id(2)

    @pl.when(t == 0)
    def _():
        prev_scr[...] = hist_ref[0]
        h_scr[...] = h0_ref[0]

    x = xr_ref[...]
    xp = jnp.concatenate([prev_scr[...], x], axis=0)
    w = cw_ref[...]
    xc = cb_ref[...] + w[0:1] * pltpu.roll(xp, 3, axis=0)[SUBLANES:]
    xc = xc + w[1:2] * pltpu.roll(xp, 2, axis=0)[SUBLANES:]
    xc = xc + w[2:3] * pltpu.roll(xp, 1, axis=0)[SUBLANES:]
    xc = xc + w[3:4] * x

    ra, gi = [], []
    for n in range(LRU_COLS // LRU_BLOCK):
        xb = xc[:, n * LRU_BLOCK:(n + 1) * LRU_BLOCK].astype(BF16)
        ra.append(jnp.dot(xb, wa_ref[n].astype(BF16), preferred_element_type=F32))
        gi.append(jnp.dot(xb, wx_ref[n].astype(BF16), preferred_element_type=F32))
    r = jax.nn.sigmoid(jnp.concatenate(ra, axis=1) + ba_ref[...])
    gin = jax.nn.sigmoid(jnp.concatenate(gi, axis=1) + bx_ref[...])
    z = -lam_ref[...]
    softplus = jnp.maximum(z, 0.0) + jnp.log1p(jnp.exp(-jnp.abs(z)))
    log_a = -LRU_C * r * softplus
    a_cum = jnp.exp(log_a)
    one_minus_a2 = -jnp.tanh(log_a) * (jnp.exp(2.0 * log_a) + 1.0)
    b_cum = jnp.sqrt(one_minus_a2) * (gin * xc)

    rows = lax.broadcasted_iota(I32, a_cum.shape, 0)
    d = 1
    while d < tc:
        keep = rows >= d
        a_sh = jnp.where(keep, pltpu.roll(a_cum, d, axis=0), 1.0)
        b_sh = jnp.where(keep, pltpu.roll(b_cum, d, axis=0), 0.0)
        b_cum = a_cum * b_sh + b_cum
        a_cum = a_sh * a_cum
        d *= 2
    h = a_cum * h_scr[...] + b_cum
    y_ref[...] = (h * jax.nn.gelu(gr_ref[...])).astype(y_ref.dtype)
    h_scr[...] = h[tc - 1:tc]
    prev_scr[...] = x[tc - SUBLANES:tc]

    @pl.when(t == pl.num_programs(2) - 1)
    def _():
        hl_ref[0] = h[tc - 1:tc]


def _rglru(proj, *, row0, n_batch, seq, tc, hist8, h0, conv_w, conv_b, wa, ba, wx, bx, lam, buf, name):
    n_tc = seq // tc
    rb0 = row0 // tc
    n_cb = D_RNN // LRU_COLS
    nb = LRU_COLS // LRU_BLOCK
    row_blk = lambda off: pl.BlockSpec((tc, LRU_COLS), lambda b, c, t, off=off: (rb0 + b * n_tc + t, off + c))
    vec = lambda: pl.BlockSpec((1, LRU_COLS), lambda b, c, t: (0, c))
    in_specs = [
        row_blk(0), row_blk(n_cb),
        pl.BlockSpec((1, SUBLANES, LRU_COLS), lambda b, c, t: (b, 0, c)),
        pl.BlockSpec((1, 1, LRU_COLS), lambda b, c, t: (b, 0, c)),
        pl.BlockSpec((CONV_W, LRU_COLS), lambda b, c, t: (0, c)), vec(),
        pl.BlockSpec((nb, LRU_BLOCK, LRU_BLOCK), lambda b, c, t: (c, 0, 0)), vec(),
        pl.BlockSpec((nb, LRU_BLOCK, LRU_BLOCK), lambda b, c, t: (c, 0, 0)), vec(), vec()]
    args = [proj, proj, hist8, h0.reshape(n_batch, 1, D_RNN), conv_w, conv_b.reshape(1, D_RNN),
            wa, ba.reshape(1, D_RNN), wx, bx.reshape(1, D_RNN), lam.reshape(1, D_RNN)]
    aliases = {}
    if buf is not None:
        in_specs.append(pl.BlockSpec(memory_space=pl.ANY))
        args.append(buf)
        aliases = {len(args) - 1: 0}
        y_shape = jax.ShapeDtypeStruct(buf.shape, buf.dtype)
        y_spec = pl.BlockSpec((tc, LRU_COLS), lambda b, c, t: (rb0 + b * n_tc + t, c))
    else:
        y_shape = jax.ShapeDtypeStruct((n_batch * seq, D_RNN), F32)
        y_spec = pl.BlockSpec((tc, LRU_COLS), lambda b, c, t: (b * n_tc + t, c))
    return pl.pallas_call(
        functools.partial(_rglru_body, tc=tc, aliased=buf is not None),
        grid=(n_batch, n_cb, n_tc),
        in_specs=in_specs,
        out_specs=[y_spec, pl.BlockSpec((1, 1, LRU_COLS), lambda b, c, t: (b, 0, c))],
        out_shape=[y_shape, jax.ShapeDtypeStruct((n_batch, 1, D_RNN), F32)],
        scratch_shapes=[pltpu.VMEM((SUBLANES, LRU_COLS), F32), pltpu.VMEM((1, LRU_COLS), F32)],
        input_output_aliases=aliases,
        compiler_params=_cparams(3), name=name,
    )(*args)


def _diff_lambda(lamv_ref, lam_init):
    lv = lamv_ref[...]
    s1 = jnp.sum(lv[0:1] * lv[1:2], axis=-1, keepdims=True)
    s2 = jnp.sum(lv[2:3] * lv[3:4], axis=-1, keepdims=True)
    return jnp.exp(s1) - jnp.exp(s2) + lam_init


ATT_TILE = 512


def _flash_body(qt_ref, k_ref, vt_ref, lamv_ref, sg_ref, buf_ref, o_ref, m_scr, l_scr, acc_scr, *, lam_init):
    i = pl.program_id(2)
    j = pl.program_id(3)

    @pl.when(j == 0)
    def _():
        m_scr[...] = jnp.full(m_scr.shape, NEG_INF, F32)
        l_scr[...] = jnp.zeros(l_scr.shape, F32)
        acc_scr[...] = jnp.zeros(acc_scr.shape, F32)

    def update(masked):
        vt = vt_ref[...]
        for c in range(2):
            hd = slice(c * HEAD_DIM, (c + 1) * HEAD_DIM)
            st = jnp.dot(k_ref[:, hd], qt_ref[hd, :], preferred_element_type=F32) * ATT_SCALE
            if masked:
                key = lax.broadcasted_iota(I32, st.shape, 0)
                query = lax.broadcasted_iota(I32, st.shape, 1)
                st = jnp.where(key <= query, st, NEG_INF)
            m_old = m_scr[c]
            m_new = jnp.maximum(m_old, jnp.max(st, axis=0, keepdims=True))
            alpha = jnp.exp(m_old - m_new)
            p = jnp.exp(st - m_new)
            l_scr[c] = alpha * l_scr[c] + jnp.sum(p, axis=0, keepdims=True)
            acc_scr[c] = alpha * acc_scr[c] + jnp.dot(vt, p.astype(BF16), preferred_element_type=F32)
            m_scr[c] = m_new

    @pl.when(j < i)
    def _():
        update(False)

    @pl.when(j == i)
    def _():
        update(True)
        lam = _diff_lambda(lamv_ref, lam_init)
        d = acc_scr[0] / l_scr[0] - lam * (acc_scr[1] / l_scr[1])
        inv = lax.rsqrt(jnp.mean(d * d, axis=0, keepdims=True) + EPS)
        o = ((d * inv) * sg_ref[...]) * (1.0 - lam_init)
        o_ref[...] = o.T.astype(o_ref.dtype)


def _flash(q_t, k_bf, v_t, lamv, subln_g, buf, lam_init):
    n_qt = SEQ // ATT_TILE
    q_spec = pl.BlockSpec((V_DIM, ATT_TILE), lambda b, h, i, j: (h, b * n_qt + i))
    k_spec = pl.BlockSpec((ATT_TILE, V_DIM), lambda b, h, i, j: (b * n_qt + jnp.minimum(i, j), h))
    v_spec = pl.BlockSpec((V_DIM, ATT_TILE), lambda b, h, i, j: (h, b * n_qt + jnp.minimum(i, j)))
    return pl.pallas_call(
        functools.partial(_flash_body, lam_init=lam_init),
        grid=(BATCH, N_HEADS, n_qt, n_qt),
        in_specs=[q_spec, k_spec, v_spec,
                  pl.BlockSpec((4, HEAD_DIM), lambda b, h, i, j: (0, 0)),
                  pl.BlockSpec((V_DIM, 1), lambda b, h, i, j: (0, 0)),
                  pl.BlockSpec(memory_space=pl.ANY)],
        out_specs=pl.BlockSpec((ATT_TILE, V_DIM), lambda b, h, i, j: (b * n_qt + i, D_RNN // V_DIM + h)),
        out_shape=jax.ShapeDtypeStruct(buf.shape, buf.dtype),
        scratch_shapes=[pltpu.VMEM((2, 1, ATT_TILE), F32), pltpu.VMEM((2, 1, ATT_TILE), F32),
                        pltpu.VMEM((2, V_DIM, ATT_TILE), F32)],
        input_output_aliases={5: 0},
        compiler_params=_cparams(4), name="flash_prompt",
    )(q_t, k_bf, v_t, lamv, subln_g.reshape(V_DIM, 1), buf)


N_HM = 2 * N_HEADS
Q_COLS = DEC_SEQ * N_HM


PAGES_PER_STEP = 8


def _paged_body(pt_ref, qt_ref, *refs, lam_init, n_steps):
    k_refs = refs[:PAGES_PER_STEP]
    v_refs = refs[PAGES_PER_STEP:2 * PAGES_PER_STEP]
    (kn_ref, vn_ref, lamv_ref, sg_ref, o_ref,
     m_scr, l_scr, acc_scr, bias_scr, p_scr) = refs[2 * PAGES_PER_STEP:]
    p = pl.program_id(1)
    k_rows = PAGE_SIZE * N_HM

    @pl.when(p == 0)
    def _():
        m_scr[...] = jnp.full(m_scr.shape, NEG_INF, F32)
        l_scr[...] = jnp.zeros(l_scr.shape, F32)
        acc_scr[...] = jnp.zeros(acc_scr.shape, F32)
        row = lax.broadcasted_iota(I32, (k_rows, Q_COLS), 0)
        col = lax.broadcasted_iota(I32, (k_rows, Q_COLS), 1)
        same_hm = jnp.bitwise_and(row, N_HM - 1) == jnp.bitwise_and(col, N_HM - 1)
        bias_scr[...] = jnp.where(same_hm, 0.0, NEG_INF)

    qt = qt_ref[0]

    def attend(blocks):
        scores = [jnp.dot(k.astype(BF16), qt, preferred_element_type=F32) * ATT_SCALE + bias
                  for k, _, bias in blocks]
        m_old = m_scr[...]
        m_new = m_old
        for st in scores:
            m_new = jnp.maximum(m_new, jnp.max(st, axis=0, keepdims=True))
        alpha = jnp.exp(m_old - m_new)
        l_new = alpha * l_scr[...]
        acc = alpha * acc_scr[...]
        for i, (st, (_, v, _)) in enumerate(zip(scores, blocks)):
            n = st.shape[0]
            pr = jnp.exp(st - m_new)
            l_new = l_new + jnp.sum(pr, axis=0, keepdims=True)
            p_scr[i, 0:n, :] = pr
            pair = p_scr[i, pl.ds(0, n // 2, stride=2), :] + p_scr[i, pl.ds(1, n // 2, stride=2), :]
            acc = acc + lax.dot_general(v.astype(BF16), pair.astype(BF16), (((0,), (0,)), ((), ())),
                                        preferred_element_type=F32)
        l_scr[...] = l_new
        m_scr[...] = m_new
        acc_scr[...] = acc

    attend([(k_ref[...], v_ref[...], bias_scr[...]) for k_ref, v_ref in zip(k_refs, v_refs)])

    @pl.when(p == n_steps - 1)
    def _():
        n_new = DEC_SEQ * N_HM
        row = lax.broadcasted_iota(I32, (n_new, Q_COLS), 0)
        col = lax.broadcasted_iota(I32, (n_new, Q_COLS), 1)
        shift = int(math.log2(N_HM))
        keep = jnp.logical_and(jnp.bitwise_and(row, N_HM - 1) == jnp.bitwise_and(col, N_HM - 1),
                               jnp.right_shift(row, shift) <= jnp.right_shift(col, shift))
        attend([(kn_ref[0], vn_ref[0], jnp.where(keep, 0.0, NEG_INF))])
        lam = _diff_lambda(lamv_ref, lam_init)
        on = acc_scr[...] / l_scr[...]
        d = on - lam * pltpu.roll(on, Q_COLS - 1, axis=1)
        inv = lax.rsqrt(jnp.mean(d * d, axis=0, keepdims=True) + EPS)
        o_ref[0] = ((d * inv) * sg_ref[...]) * (1.0 - lam_init)


def _paged(qt_s, cache_k, cache_v, layer, page_table, k_new, v_new, lamv, subln_g, lam_init):
    n_pages = page_table.shape[1]
    assert n_pages % PAGES_PER_STEP == 0
    n_steps = n_pages // PAGES_PER_STEP
    k_rows, v_rows = cache_k.shape[2], cache_v.shape[2]

    def page_spec(rows, width, i):
        return pl.BlockSpec((None, None, rows, width),
                            lambda b, p, pt, i=i: (layer, pt[b * n_pages + p * PAGES_PER_STEP + i], 0, 0))

    per_batch = lambda shape: pl.BlockSpec((1,) + shape, lambda b, p, pt: (b, 0, 0))
    return pl.pallas_call(
        functools.partial(_paged_body, lam_init=lam_init, n_steps=n_steps),
        grid_spec=pltpu.PrefetchScalarGridSpec(
            num_scalar_prefetch=1,
            grid=(DEC_BATCH, n_steps),
            in_specs=[per_batch((HEAD_DIM, Q_COLS))]
            + [page_spec(k_rows, HEAD_DIM, i) for i in range(PAGES_PER_STEP)]
            + [page_spec(v_rows, V_DIM, i) for i in range(PAGES_PER_STEP)]
            + [per_batch((DEC_SEQ * N_HM, HEAD_DIM)), per_batch((DEC_SEQ * N_HEADS, V_DIM)),
               pl.BlockSpec((4, HEAD_DIM), lambda b, p, pt: (0, 0)),
               pl.BlockSpec((V_DIM, 1), lambda b, p, pt: (0, 0))],
            out_specs=per_batch((V_DIM, Q_COLS)),
            scratch_shapes=[pltpu.VMEM((1, Q_COLS), F32), pltpu.VMEM((1, Q_COLS), F32),
                            pltpu.VMEM((V_DIM, Q_COLS), F32),
                            pltpu.VMEM((k_rows, Q_COLS), F32),
                            pltpu.VMEM((PAGES_PER_STEP, k_rows, Q_COLS), F32)]),
        out_shape=jax.ShapeDtypeStruct((DEC_BATCH, V_DIM, Q_COLS), F32),
        compiler_params=_cparams(2), name="paged_sample",
    )(page_table.reshape(-1), qt_s, *([cache_k] * PAGES_PER_STEP), *([cache_v] * PAGES_PER_STEP),
      k_new, v_new, lamv, subln_g.reshape(V_DIM, 1))


def _place_body(rows_ref, buf_ref, o_ref):
    o_ref[...] = rows_ref[...]


def _place_rows(buf, rows, row0):
    n = rows.shape[0]
    assert row0 % n == 0
    return pl.pallas_call(
        _place_body,
        grid=(1,),
        in_specs=[pl.BlockSpec(rows.shape, lambda i: (0, 0)), pl.BlockSpec(memory_space=pl.ANY)],
        out_specs=pl.BlockSpec(rows.shape, lambda i: (row0 // n, 0)),
        out_shape=jax.ShapeDtypeStruct(buf.shape, buf.dtype),
        input_output_aliases={1: 0},
        compiler_params=_cparams(1), name="place_rows",
    )(rows, buf)


N_ASSIGN = TOP_K * M_REAL
MOE_TILES = (N_ASSIGN + N_EXPERTS * (TOKEN_TILE - 1)) // TOKEN_TILE
MOE_ROWS = MOE_TILES * TOKEN_TILE
GATHER_TILE = 512
COMBINE_TILE = 128


def _row_copy(src_hbm, row, dst_vmem, slot, sem):
    return pltpu.make_async_copy(src_hbm.at[pl.ds(row, 1), :], dst_vmem.at[pl.ds(slot, 1), :], sem)


DMA_ISSUE_UNROLL = 8


def _gather_body(src_ref, x_hbm, o_ref, buf, sem):
    def start(g, carry):
        for u in range(DMA_ISSUE_UNROLL):
            r = g * DMA_ISSUE_UNROLL + u
            _row_copy(x_hbm, src_ref[0, r], buf, r, sem).start(priority=u % 2)
        return carry

    lax.fori_loop(0, GATHER_TILE // DMA_ISSUE_UNROLL, start, 0)
    for r in range(GATHER_TILE):
        _row_copy(x_hbm, 0, buf, r, sem).wait()
    o_ref[...] = buf[...].astype(BF16)


def _gather_rows(src_rows, x32):
    n_t = MOE_ROWS // GATHER_TILE
    return pl.pallas_call(
        _gather_body,
        grid=(n_t,),
        in_specs=[pl.BlockSpec((None, 1, GATHER_TILE), lambda i: (i, 0, 0), memory_space=pltpu.SMEM),
                  pl.BlockSpec(memory_space=pl.ANY)],
        out_specs=pl.BlockSpec((GATHER_TILE, D_MODEL), lambda i: (i, 0)),
        out_shape=jax.ShapeDtypeStruct((MOE_ROWS, D_MODEL), BF16),
        scratch_shapes=[pltpu.VMEM((GATHER_TILE, D_MODEL), F32), pltpu.SemaphoreType.DMA(())],
        compiler_params=_cparams(1), name="moe_gather",
    )(src_rows.reshape(n_t, 1, GATHER_TILE), x32)


def _combine_body(pos_ref, ys_hbm, g_ref, o_ref, buf, sem):
    def start(g, carry):
        for u in range(DMA_ISSUE_UNROLL):
            r = g * DMA_ISSUE_UNROLL + u
            for k in range(TOP_K):
                _row_copy(ys_hbm, pos_ref[k, r], buf.at[k], r, sem).start(priority=k)
        return carry

    lax.fori_loop(0, COMBINE_TILE // DMA_ISSUE_UNROLL, start, 0)
    for r in range(COMBINE_TILE):
        for k in range(TOP_K):
            _row_copy(ys_hbm, 0, buf.at[k], r, sem).wait()
    g = g_ref[...]
    o_ref[...] = g[:, 0:1] * buf[0] + g[:, 1:2] * buf[1]


def _combine_rows(pos, gates_t, ys):
    n_t = M_PAD // COMBINE_TILE
    pos_blk = pos.reshape(TOP_K, n_t, COMBINE_TILE).transpose(1, 0, 2)
    return pl.pallas_call(
        _combine_body,
        grid=(n_t,),
        in_specs=[pl.BlockSpec((None, TOP_K, COMBINE_TILE), lambda i: (i, 0, 0), memory_space=pltpu.SMEM),
                  pl.BlockSpec(memory_space=pl.ANY),
                  pl.BlockSpec((COMBINE_TILE, TOP_K), lambda i: (i, 0))],
        out_specs=pl.BlockSpec((COMBINE_TILE, D_MODEL), lambda i: (i, 0)),
        out_shape=jax.ShapeDtypeStruct((M_PAD, D_MODEL), F32),
        scratch_shapes=[pltpu.VMEM((TOP_K, COMBINE_TILE, D_MODEL), F32), pltpu.SemaphoreType.DMA(())],
        compiler_params=_cparams(1), name="moe_combine",
    )(pos_blk, ys, gates_t)


def _moe_plan(idx):
    e_a = idx[:, :M_REAL].reshape(-1)
    onehot = (e_a[:, None] == jnp.arange(N_EXPERTS, dtype=I32)[None, :]).astype(I32)
    running = jnp.cumsum(onehot, axis=0).astype(I32)
    counts = running[-1]
    rank = jnp.sum((running - onehot) * onehot, axis=1)
    tiles_e = (counts + TOKEN_TILE - 1) // TOKEN_TILE
    tile_end = jnp.cumsum(tiles_e).astype(I32)
    tile_start = tile_end - tiles_e
    dest = tile_start[e_a] * TOKEN_TILE + rank
    pos = jnp.pad(dest.reshape(TOP_K, M_REAL), ((0, 0), (0, M_PAD - M_REAL)))
    tl = jnp.arange(MOE_TILES, dtype=I32)
    tl_used = jnp.minimum(tl, tile_end[-1] - 1)
    tile_expert = jnp.minimum(jnp.searchsorted(tile_end, tl_used, side="right").astype(I32), N_EXPERTS - 1)
    tile_valid = (tl < tile_end[-1]).astype(I32)
    tile_first = jnp.logical_and(tile_valid > 0, tl == tile_start[tile_expert]).astype(I32)
    order = jnp.argsort(e_a, stable=True).astype(I32)
    first_sorted = (jnp.cumsum(counts) - counts).astype(I32)
    row = jnp.arange(MOE_ROWS, dtype=I32)
    row_tile = row // TOKEN_TILE
    row_expert = tile_expert[row_tile]
    offset = row - tile_start[row_expert] * TOKEN_TILE
    used = jnp.logical_and(tile_valid[row_tile] > 0, offset < counts[row_expert])
    assignment = order[jnp.clip(first_sorted[row_expert] + offset, 0, N_ASSIGN - 1)]
    src_rows = jnp.where(used, assignment % M_REAL, 0).astype(I32)
    return src_rows, pos, tile_valid, tile_expert, tile_first


PROJ_TN, PROJ_KC = 1024, 1024
GU_TN, GU_KC = 512, 1024
DOWN_TM, DOWN_TN, DOWN_KC = 256, 512, 2048


def _ffn(h_rows, wg, wu, wd, sched_gu, sched_down, tag):
    hmid = _matmul(h_rows, [wg, wu], sched_gu, tm=TOKEN_TILE, tn=GU_TN, kc=GU_KC, out_dtype=BF16,
                   swiglu=True, name=tag + "_gate_up")
    return _matmul(hmid, [wd], sched_down, tm=DOWN_TM, tn=DOWN_TN, kc=DOWN_KC, out_dtype=F32,
                   name=tag + "_down")


def _split_tiles(valid, expert, first, parts):
    sub_first = jnp.stack([first] + [jnp.zeros_like(first)] * (parts - 1), axis=1).reshape(-1)
    return jnp.repeat(valid, parts), jnp.repeat(expert, parts), sub_first


def kernel(x_prompt, x_sample, c_prompt, c_sample, cache_k, cache_v, page_table, state_h, state_conv, w_in, conv_w, conv_b, lru_wa, lru_ba, lru_wx, lru_bx, lru_lambda, lambda_q1, lambda_k1, lambda_q2, lambda_k2, subln_g, w_out, w_mod, b_mod, g_pre_mix, g_post_mix, g_pre_ffn, g_post_ffn, w_gate, w_up, w_down, router_w, router_b, we_gate, we_up, we_down):
    n_pages = page_table.shape[1]
    past_len = n_pages * cache_k.shape[2]

    x = jnp.concatenate([x_prompt.reshape(M_PROMPT, D_MODEL), x_sample.reshape(M_SAMPLE, D_MODEL),
                         jnp.zeros((M_PAD - M_REAL, D_MODEL), F32)], axis=0)
    c_rows = jnp.concatenate([c_prompt, c_sample,
                              jnp.zeros((N_MOD_ROWS - BATCH - DEC_BATCH, D_MODEL), F32)], axis=0)

    pos_rows = np.concatenate([np.tile(np.arange(SEQ), BATCH),
                               np.tile(past_len + np.arange(DEC_SEQ), DEC_BATCH),
                               np.zeros((M_PAD - M_REAL,))]).astype(np.float32)
    half = HEAD_DIM // 2
    inv_freq = ROPE_THETA ** (-jnp.arange(half, dtype=F32) / half)
    ang = jnp.asarray(pos_rows)[:, None] * inv_freq[None, :]
    cos_tab = jnp.concatenate([jnp.cos(ang), jnp.cos(ang)], axis=1)
    sin_tab = jnp.concatenate([-jnp.sin(ang), jnp.sin(ang)], axis=1)

    n_tok_tiles = M_PAD // TOKEN_TILE
    sched_k4096 = [_dense_schedule(n_tok_tiles, D_MODEL // PROJ_KC, l) for l in range(DEPTH)]
    sched_gu = [_dense_schedule(n_tok_tiles, D_MODEL // GU_KC, i) for i in range(w_gate.shape[0])]
    sched_down = [_dense_schedule(M_PAD // DOWN_TM, D_FF // DOWN_KC, i) for i in range(w_down.shape[0])]
    sched_mod = [_dense_schedule(1, D_MODEL // PROJ_KC, l) for l in range(DEPTH)]

    mods = [_matmul(c_rows, [w_mod], sched_mod[l], tm=N_MOD_ROWS, tn=PROJ_TN, kc=PROJ_KC,
                    out_dtype=F32, bias=b_mod[l], x_silu=True, name=f"mod{l}") for l in range(DEPTH)]
    moe_w = [w.reshape((-1,) + w.shape[2:]) for w in (we_gate, we_up, we_down)]

    hm = _norm(x, mods[0], g_next=g_pre_mix[0], shift_col=0, scale_col=1, name="pre_mix0")[0]

    zeros_hist = jnp.zeros((BATCH, SUBLANES, D_RNN), F32)
    zeros_h0 = jnp.zeros((BATCH, D_RNN), F32)
    cache_k2 = cache_k.reshape(cache_k.shape[0], cache_k.shape[1], -1, HEAD_DIM)
    cache_v2 = cache_v.reshape(cache_v.shape[0], cache_v.shape[1], -1, V_DIM)

    ks, vs, hs, cs = [], [], [], []
    for l in range(DEPTH):
        mod = mods[l]
        lam_init = 0.8 - 0.6 * math.exp(-0.3 * l)
        proj = _matmul(hm, [w_in], sched_k4096[l], tm=TOKEN_TILE, tn=PROJ_TN, kc=PROJ_KC,
                       out_dtype=F32, name=f"w_in{l}")
        q_bf, k_f32, k_bf, v_bf = _rope(proj, cos_tab, sin_tab)

        lru = dict(conv_w=conv_w[l], conv_b=conv_b[l], wa=lru_wa[l], ba=lru_ba[l], wx=lru_wx[l],
                   bx=lru_bx[l], lam=lru_lambda[l])
        buf = jnp.zeros((M_PAD, D_MODEL), BF16)
        buf, h_last_p = _rglru(proj, row0=0, n_batch=BATCH, seq=SEQ, tc=256, hist8=zeros_hist,
                               h0=zeros_h0, buf=buf, name=f"rglru_prompt{l}", **lru)
        hist_s = jnp.pad(state_conv[l], ((0, 0), (SUBLANES - (CONV_W - 1), 0), (0, 0)))
        y_s, h_last_s = _rglru(proj, row0=M_PROMPT, n_batch=DEC_BATCH, seq=DEC_SEQ, tc=DEC_SEQ,
                               hist8=hist_s, h0=state_h[l], buf=None, name=f"rglru_sample{l}", **lru)

        lamv = jnp.stack([lambda_q1[l], lambda_k1[l], lambda_q2[l], lambda_k2[l]], axis=0)
        buf = _flash(q_bf.T, k_bf, v_bf.T, lamv, subln_g[l], buf, lam_init)
        q_s = q_bf[M_PROMPT:M_REAL].reshape(DEC_BATCH, Q_COLS, HEAD_DIM)
        k_s = k_f32[M_PROMPT:M_REAL].reshape(DEC_BATCH, DEC_SEQ * N_HM, HEAD_DIM)
        v_s = proj[M_PROMPT:M_REAL, 2 * D_RNN + 2 * D_QK:].reshape(DEC_BATCH, DEC_SEQ * N_HEADS, V_DIM)
        o_t = _paged(jnp.swapaxes(q_s, 1, 2), cache_k2, cache_v2, l, page_table, k_s, v_s,
                     lamv, subln_g[l], lam_init)
        o_s = jnp.transpose(o_t.reshape(DEC_BATCH, V_DIM, DEC_SEQ, N_HEADS, 2)[..., 0], (0, 2, 3, 1))
        mix_s = jnp.concatenate([y_s, o_s.reshape(M_SAMPLE, D_ATT)], axis=1).astype(BF16)
        buf = _place_rows(buf, mix_s, M_PROMPT)

        m = _matmul(buf, [w_out], sched_k4096[l], tm=TOKEN_TILE, tn=PROJ_TN, kc=PROJ_KC,
                    out_dtype=F32, name=f"w_out{l}")
        i = l // 2
        if l % 2 == 0:
            x, hf = _norm(x, mod, f=m, g_post=g_post_mix[l], gate_col=2, g_next=g_pre_ffn[l],
                          shift_col=3, scale_col=4, name=f"post_mix{l}")
            f = _ffn(hf, w_gate, w_up, w_down, sched_gu[i], sched_down[i], f"ffn{l}")
        else:
            x, hf, hf32, idx, gates = _norm(x, mod, f=m, g_post=g_post_mix[l], gate_col=2,
                                            g_next=g_pre_ffn[l], shift_col=3, scale_col=4,
                                            router_w=router_w[i], router_b=router_b[i],
                                            name=f"post_mix{l}")
            src_rows, pos, t_valid, t_expert, t_first = _moe_plan(idx)
            t_expert = t_expert + i * N_EXPERTS
            xs = _gather_rows(src_rows, hf32)
            nkc_gu, nkc_dn = D_MODEL // GU_KC, D_FF // DOWN_KC
            moe_gu = _build_schedule(t_valid, t_expert, t_first, nkc_gu,
                                     MOE_TILES + N_EXPERTS * (nkc_gu - 1))
            parts = TOKEN_TILE // DOWN_TM
            moe_dn = _build_schedule(*_split_tiles(t_valid, t_expert, t_first, parts), nkc_dn,
                                     MOE_TILES * parts + N_EXPERTS * (nkc_dn - 1))
            ys = _ffn(xs, moe_w[0], moe_w[1], moe_w[2], moe_gu, moe_dn, f"moe{l}")
            gates_t = jnp.where(jnp.arange(M_PAD)[:, None] < M_REAL, gates.T, 0.0)
            f = _combine_rows(pos, gates_t, ys)
        if l + 1 < DEPTH:
            x, hm = _norm(x, mod, f=f, g_post=g_post_ffn[l], gate_col=5, g_next=g_pre_mix[l + 1],
                          shift_col=0, scale_col=1, mod_next=mods[l + 1], name=f"post_ffn{l}")
        else:
            x = _norm(x, mod, f=f, g_post=g_post_ffn[l], gate_col=5, name=f"post_ffn{l}")[0]

        ks.append(k_f32)
        vs.append(proj[:, 2 * D_RNN + 2 * D_QK:])
        hs.append((h_last_p, h_last_s))
        tail = CONV_W - 1
        cs.append((jnp.stack([proj[(b + 1) * SEQ - tail:(b + 1) * SEQ, :D_RNN] for b in range(BATCH)]),
                   jnp.stack([proj[M_PROMPT + (b + 1) * DEC_SEQ - tail:M_PROMPT + (b + 1) * DEC_SEQ, :D_RNN]
                              for b in range(DEC_BATCH)])))

    def prompt_rows(a):
        return a[:M_PROMPT]

    def sample_rows(a):
        return a[M_PROMPT:M_REAL]

    y_prompt = prompt_rows(x).reshape(BATCH, SEQ, D_MODEL)
    y_sample = sample_rows(x).reshape(DEC_BATCH, DEC_SEQ, D_MODEL)
    k_prompt = jnp.stack([prompt_rows(k).reshape(BATCH, SEQ, N_HEADS, 2, HEAD_DIM) for k in ks])
    k_sample = jnp.stack([sample_rows(k).reshape(DEC_BATCH, DEC_SEQ, N_HEADS, 2, HEAD_DIM) for k in ks])
    v_prompt = jnp.stack([prompt_rows(v).reshape(BATCH, SEQ, N_HEADS, V_DIM) for v in vs])
    v_sample = jnp.stack([sample_rows(v).reshape(DEC_BATCH, DEC_SEQ, N_HEADS, V_DIM) for v in vs])
    h_prompt = jnp.stack([h[0].reshape(BATCH, D_RNN) for h in hs])
    h_sample = jnp.stack([h[1].reshape(DEC_BATCH, D_RNN) for h in hs])
    conv_prompt = jnp.stack([c[0] for c in cs])
    conv_sample = jnp.stack([c[1] for c in cs])
    return (y_prompt, y_sample, k_prompt, v_prompt, h_prompt, conv_prompt,
            k_sample, v_sample, h_sample, conv_sample)
```
